```python
import jax, jax.numpy as jnp
from jax import lax
import numpy as np

D_MODEL = 1024
BATCH = 2
SEQ = 8192
DEPTH = 2

N_MIXERS = 2
GRID_W = 64
EPS = 1e-6

ATT_HEADS = 8
ATT_KV_HEADS = 2
ATT_HEAD_DIM = 128
ATT_GROUP = ATT_HEADS // ATT_KV_HEADS
ATT_Q_BLOCK = 128
ROPE_THETA = 10000.0
ATT_Q_W = ATT_HEADS * ATT_HEAD_DIM
ATT_KV_W = ATT_KV_HEADS * ATT_HEAD_DIM
ATT_IN_W = 2 * ATT_Q_W + 2 * ATT_KV_W

GDN_QK_HEADS = 8
GDN_V_HEADS = 16
GDN_DK = 128
GDN_DV = 128
GDN_CONV = 5
GDN_CHUNK = 64
GDN_QK_W = GDN_QK_HEADS * GDN_DK
GDN_V_W = GDN_V_HEADS * GDN_DV
GDN_CONV_W = 2 * GDN_QK_W + GDN_V_W
GDN_IN_W = GDN_CONV_W + GDN_V_W + 4 * GDN_V_HEADS

D_FF = 2816
FFN_CONV = 3

kernel_name = "hybrid_axial_gqa_gated_deltanet_convffn_encoder"


def rmsnorm(x, g):
    xf = x.astype(jnp.float32)
    y = xf * lax.rsqrt(jnp.mean(xf * xf, axis=-1, keepdims=True) + EPS)
    return (y * g.astype(jnp.float32)).astype(x.dtype)


def l2norm(x):
    return x * lax.rsqrt(jnp.sum(x * x, axis=-1, keepdims=True) + EPS)


def depthwise_conv_centred(x, w, b):
    K = w.shape[0]
    S = x.shape[1]
    pad = K // 2
    xp = jnp.pad(x, ((0, 0), (pad, pad), (0, 0)))
    y = xp[:, 0:S] * w[0]
    for t in range(1, K):
        y = y + xp[:, t:t + S] * w[t]
    return y + b


def axial_rope_tables(S):
    rows = S // GRID_W
    row = jnp.repeat(jnp.arange(rows), GRID_W).astype(jnp.float32)
    col = jnp.tile(jnp.arange(GRID_W), rows).astype(jnp.float32)
    half = ATT_HEAD_DIM // 2
    n_freq = half // 2
    inv_freq = ROPE_THETA ** (-(jnp.arange(n_freq, dtype=jnp.float32) * 2.0 / half))
    ang_r = row[:, None] * inv_freq[None, :]
    ang_c = col[:, None] * inv_freq[None, :]
    return (jnp.cos(ang_r), jnp.sin(ang_r), jnp.cos(ang_c), jnp.sin(ang_c))


def _rotate(z, cos, sin):
    n = cos.shape[-1]
    z1, z2 = z[..., :n], z[..., n:]
    c = cos[None, :, None, :]
    s = sin[None, :, None, :]
    return jnp.concatenate([z1 * c - z2 * s, z1 * s + z2 * c], axis=-1)


def apply_axial_rope(x, rope):
    cos_r, sin_r, cos_c, sin_c = rope
    half = ATT_HEAD_DIM // 2
    return jnp.concatenate([_rotate(x[..., :half], cos_r, sin_r),
                            _rotate(x[..., half:], cos_c, sin_c)], axis=-1)


def attention_mixer(h, w_in, q_norm, k_norm, w_out, rope):
    B, S, _ = h.shape
    proj = h @ w_in
    q, k, v, gate = jnp.split(proj, [ATT_Q_W, ATT_Q_W + ATT_KV_W, ATT_Q_W + 2 * ATT_KV_W], axis=-1)
    q = q.reshape(B, S, ATT_HEADS, ATT_HEAD_DIM)
    k = k.reshape(B, S, ATT_KV_HEADS, ATT_HEAD_DIM)
    v = v.reshape(B, S, ATT_KV_HEADS, ATT_HEAD_DIM).astype(jnp.float32)
    q = apply_axial_rope(rmsnorm(q, q_norm).astype(jnp.float32), rope) * (ATT_HEAD_DIM ** -0.5)
    k = apply_axial_rope(rmsnorm(k, k_norm).astype(jnp.float32), rope)
    nb = S // ATT_Q_BLOCK
    qb = q.reshape(B, nb, ATT_Q_BLOCK, ATT_KV_HEADS, ATT_GROUP, ATT_HEAD_DIM)
    qb = jnp.moveaxis(qb, 1, 0)

    def block(q_blk):
        s = jnp.einsum('bqkgd,bskd->bkgqs', q_blk, k)
        p = jax.nn.softmax(s, axis=-1)
        return jnp.einsum('bkgqs,bskd->bqkgd', p, v)

    o = lax.map(block, qb)
    o = jnp.moveaxis(o, 0, 1).reshape(B, S, ATT_Q_W)
    o = o * jax.nn.sigmoid(gate.astype(jnp.float32))
    return o.astype(h.dtype) @ w_out


def chunk_gated_delta_rule(q, k, v, g, beta):
    B, S, H, dk = q.shape
    dv = v.shape[-1]
    C = GDN_CHUNK
    N = S // C

    def chunks(t):
        t = t.reshape((B, N, C, H) + t.shape[3:])
        return jnp.moveaxis(t, 3, 1)

    q, k, v, g, beta = chunks(q), chunks(k), chunks(v), chunks(g), chunks(beta)
    gc = jnp.cumsum(g, axis=-1)
    idx = jnp.arange(C)
    incl = idx[:, None] >= idx[None, :]
    strict = idx[:, None] > idx[None, :]
    decay = jnp.exp(jnp.where(incl, gc[..., :, None] - gc[..., None, :], -jnp.inf))
    kb = k * beta[..., None]
    L = jnp.einsum('bhncd,bhnsd->bhncs', kb, k) * decay * strict.astype(jnp.float32)
    unit_lower = L + jnp.eye(C, dtype=jnp.float32)
    rhs = jnp.concatenate([v * beta[..., None], kb * jnp.exp(gc)[..., None]], axis=-1)
    sol = lax.linalg.triangular_solve(unit_lower, rhs, left_side=True, lower=True,
                                      unit_diagonal=True)
    u, w = sol[..., :dv], sol[..., dv:]
    attn = jnp.einsum('bhncd,bhnsd->bhncs', q, k) * decay
    q_dec = q * jnp.exp(gc)[..., None]
    k_dec = k * jnp.exp(gc[..., -1:] - gc)[..., None]
    g_last = jnp.exp(gc[..., -1])

    xs = tuple(jnp.moveaxis(t, 2, 0) for t in (u, w, attn, q_dec, k_dec, g_last))

    def step(state, inp):
        u_i, w_i, a_i, qd_i, kd_i, gl_i = inp
        v_new = u_i - jnp.einsum('bhck,bhkv->bhcv', w_i, state)
        o_i = jnp.einsum('bhck,bhkv->bhcv', qd_i, state) + jnp.einsum('bhcs,bhsv->bhcv', a_i, v_new)
        state = state * gl_i[..., None, None] + jnp.einsum('bhck,bhcv->bhkv', kd_i, v_new)
        return state, o_i

    state0 = jnp.zeros((B, H, dk, dv), jnp.float32)
    _, o = lax.scan(step, state0, xs)
    return jnp.transpose(o, (1, 0, 3, 2, 4)).reshape(B, S, H, dv)


def gated_deltanet_mixer(h, w_in, conv_w, conv_b, a_log, dt_bias, o_norm, w_out):
    B, S, _ = h.shape
    proj = h @ w_in
    qkv, z, ab = jnp.split(proj, [GDN_CONV_W, GDN_CONV_W + GDN_V_W], axis=-1)
    qkv = jax.nn.silu(depthwise_conv_centred(qkv, conv_w, conv_b)).astype(jnp.float32)
    q, k, v = jnp.split(qkv, [GDN_QK_W, 2 * GDN_QK_W], axis=-1)
    rep = GDN_V_HEADS // GDN_QK_HEADS
    q = jnp.repeat(l2norm(q.reshape(B, S, GDN_QK_HEADS, GDN_DK)) * (GDN_DK ** -0.5), rep, axis=2)
    k = jnp.repeat(l2norm(k.reshape(B, S, GDN_QK_HEADS, GDN_DK)), rep, axis=2)
    v = v.reshape(B, S, GDN_V_HEADS, GDN_DV)
    ab = ab.astype(jnp.float32).reshape(B, S, 2, 2, GDN_V_HEADS)
    a, b = ab[:, :, :, 0], ab[:, :, :, 1]
    g = -jnp.exp(a_log.astype(jnp.float32)) * jax.nn.softplus(a + dt_bias.astype(jnp.float32))
    beta = jax.nn.sigmoid(b)
    o_fwd = chunk_gated_delta_rule(q, k, v, g[:, :, 0], beta[:, :, 0])
    flip = lambda t: jnp.flip(t, axis=1)
    o_bwd = flip(chunk_gated_delta_rule(flip(q), flip(k), flip(v),
                                        flip(g[:, :, 1]), flip(beta[:, :, 1])))
    o = rmsnorm(o_fwd + o_bwd, o_norm) * jax.nn.silu(
        z.astype(jnp.float32).reshape(B, S, GDN_V_HEADS, GDN_DV))
    return o.reshape(B, S, GDN_V_W).astype(h.dtype) @ w_out


def conv_ffn(h, w_up, conv_w, conv_b, w_down):
    u = depthwise_conv_centred(h @ w_up, conv_w, conv_b)
    a, b = jnp.split(u, 2, axis=-1)
    return (jax.nn.silu(a) * b) @ w_down


def setup_inputs(seed: int = 0) -> dict:
    key = jax.random.key(seed)
    ks = jax.random.split(key, 24)
    n_attn = (DEPTH + N_MIXERS - 1) // N_MIXERS
    n_gdn = DEPTH // N_MIXERS
    f32 = jnp.float32

    def w(k, shape, fan_in):
        return jax.random.normal(k, shape, f32) * (fan_in ** -0.5)

    def gain(k, shape):
        return 1.0 + 0.02 * jax.random.normal(k, shape, f32)

    a_log = jnp.log(jax.random.uniform(ks[12], (n_gdn, 2, GDN_V_HEADS), f32, 1.0, 16.0))
    dt = jnp.exp(jax.random.uniform(ks[13], (n_gdn, 2, GDN_V_HEADS), f32,
                                    float(np.log(1e-3)), float(np.log(1e-1))))
    dt_bias = dt + jnp.log(-jnp.expm1(-dt))
    return {
        "x": jax.random.normal(ks[0], (BATCH, SEQ, D_MODEL), f32),
        "norm_mix": gain(ks[1], (DEPTH, D_MODEL)),
        "norm_ffn": gain(ks[2], (DEPTH, D_MODEL)),
        "norm_final": gain(ks[3], (D_MODEL,)),
        "attn_w_in": w(ks[4], (n_attn, D_MODEL, ATT_IN_W), D_MODEL),
        "attn_q_norm": gain(ks[5], (n_attn, ATT_HEAD_DIM)),
        "attn_k_norm": gain(ks[6], (n_attn, ATT_HEAD_DIM)),
        "attn_w_out": w(ks[7], (n_attn, ATT_Q_W, D_MODEL), ATT_Q_W),
        "gdn_w_in": w(ks[8], (n_gdn, D_MODEL, GDN_IN_W), D_MODEL),
        "gdn_conv_w": w(ks[9], (n_gdn, GDN_CONV, GDN_CONV_W), GDN_CONV),
        "gdn_conv_b": 0.02 * jax.random.normal(ks[10], (n_gdn, GDN_CONV_W), f32),
        "gdn_a_log": a_log,
        "gdn_dt_bias": dt_bias,
        "gdn_o_norm": gain(ks[14], (n_gdn, GDN_DV)),
        "gdn_w_out": w(ks[15], (n_gdn, GDN_V_W, D_MODEL), GDN_V_W),
        "ffn_w_up": w(ks[16], (DEPTH, D_MODEL, 2 * D_FF), D_MODEL),
        "ffn_conv_w": w(ks[17], (DEPTH, FFN_CONV, 2 * D_FF), FFN_CONV),
        "ffn_conv_b": 0.02 * jax.random.normal(ks[18], (DEPTH, 2 * D_FF), f32),
        "ffn_w_down": w(ks[19], (DEPTH, D_FF, D_MODEL), D_FF),
    }


def reference(x, norm_mix, norm_ffn, norm_final, attn_w_in, attn_q_norm, attn_k_norm,
              attn_w_out, gdn_w_in, gdn_conv_w, gdn_conv_b, gdn_a_log, gdn_dt_bias,
              gdn_o_norm, gdn_w_out, ffn_w_up, ffn_conv_w, ffn_conv_b, ffn_w_down):
    S = x.shape[1]
    rope = axial_rope_tables(S)
    h = x
    for i in range(DEPTH):
        j = i // N_MIXERS
        y = rmsnorm(h, norm_mix[i])
        if i % N_MIXERS == 0:
            h = h + attention_mixer(y, attn_w_in[j], attn_q_norm[j], attn_k_norm[j],
                                    attn_w_out[j], rope)
        else:
            h = h + gated_deltanet_mixer(y, gdn_w_in[j], gdn_conv_w[j], gdn_conv_b[j],
                                         gdn_a_log[j], gdn_dt_bias[j], gdn_o_norm[j],
                                         gdn_w_out[j])
        h = h + conv_ffn(rmsnorm(h, norm_ffn[i]), ffn_w_up[i], ffn_conv_w[i],
                         ffn_conv_b[i], ffn_w_down[i])
    return rmsnorm(h, norm_final)
```

```python
import functools

import jax
import jax.numpy as jnp
from jax import lax
from jax.experimental import pallas as pl
from jax.experimental.pallas import tpu as pltpu

F32 = jnp.float32
BF16 = jnp.bfloat16

EPS = 1e-6
ROPE_THETA = 10000.0
ROPE_GRID_W = 64

HEAD_DIM = 128
ATT_HEADS = 8
ATT_KV_HEADS = 2
GDN_QK_HEADS = 8
GDN_V_HEADS = 16
GDN_CHUNK = 128
GDN_DIAG = 16

HALO = 8
MXU_N = 256
V7X_VMEM_LIMIT = 56 * 1024 * 1024

TM_PROJ = 512
TM_GDN_IN = 256
ATT_TQ = 256
ATT_TK = 1024


def _dot(a, b):
    return jnp.dot(a, b, preferred_element_type=F32)


def _dot_nt(a, b):
    return lax.dot_general(a, b, (((1,), (1,)), ((), ())), preferred_element_type=F32)


def _rms(x, gain):
    return x * lax.rsqrt(jnp.mean(x * x, axis=-1, keepdims=True) + EPS) * gain


def _sigmoid(x):
    return 1.0 / (1.0 + jnp.exp(-x))


def _const_spec(shape):
    return pl.BlockSpec(shape, lambda *_: (0,) * len(shape), pipeline_mode=pl.Buffered(1))


def _params(*semantics):
    return pltpu.CompilerParams(dimension_semantics=semantics, vmem_limit_bytes=V7X_VMEM_LIMIT)


def _halo_specs(tm, d, n_tiles):
    r = tm // HALO
    last = n_tiles * r - 1
    return [
        pl.BlockSpec((tm, d), lambda i: (i, 0)),
        pl.BlockSpec((HALO, d), lambda i: (jnp.maximum(i * r - 1, 0), 0)),
        pl.BlockSpec((HALO, d), lambda i: (jnp.minimum((i + 1) * r, last), 0)),
    ]


def _normed_with_halo(x_ref, xp_ref, xn_ref, gain, seq_tiles):
    i = pl.program_id(0)
    pos = i % seq_tiles
    y = _rms(x_ref[...], gain)
    yn = jnp.where(pos == seq_tiles - 1, 0.0, _rms(xn_ref[...], gain))
    yp = jnp.where(pos == 0, 0.0, _rms(xp_ref[...], gain))
    return jnp.concatenate([y, yn, yp], axis=0).astype(BF16)


def _seq_conv(u, w_ref, b_ref, cols, tm):
    rows = u.shape[0]
    taps = w_ref.shape[0]
    half = taps // 2
    acc = u * w_ref[half:half + 1, cols]
    for t in range(taps):
        if t == half:
            continue
        shift = (half - t) % rows
        acc = acc + pltpu.roll(u, shift, 0) * w_ref[t:t + 1, cols]
    return acc[0:tm] + b_ref[:, cols]


def _attn_inproj_kernel(x_ref, g_ref, w_ref, qg_ref, kg_ref, cos_ref, sa_ref, sb_ref,
                        q_ref, k_ref, v_ref, gate_ref):
    hd = HEAD_DIM
    yb = _rms(x_ref[...], g_ref[...]).astype(BF16)
    cos, sa, sb = cos_ref[...], sa_ref[...], sb_ref[...]
    q_w = ATT_HEADS * hd
    kv_w = ATT_KV_HEADS * hd

    def norm_rope(z, gain):
        zn = _rms(z, gain)
        return zn * cos + pltpu.roll(zn, hd - hd // 4, 1) * sa + pltpu.roll(zn, hd // 4, 1) * sb

    for c in range(q_w // MXU_N):
        z = _dot(yb, w_ref[:, c * MXU_N:(c + 1) * MXU_N])
        for e in range(MXU_N // hd):
            col = c * MXU_N + e * hd
            q_ref[:, col:col + hd] = norm_rope(z[:, e * hd:(e + 1) * hd], qg_ref[...]).astype(BF16)
    for c in range(kv_w // MXU_N):
        z = _dot(yb, w_ref[:, q_w + c * MXU_N:q_w + (c + 1) * MXU_N])
        for e in range(MXU_N // hd):
            col = c * MXU_N + e * hd
            k_ref[:, col:col + hd] = norm_rope(z[:, e * hd:(e + 1) * hd], kg_ref[...]).astype(BF16)
    for c in range(kv_w // MXU_N):
        lo = q_w + kv_w + c * MXU_N
        v_ref[:, c * MXU_N:(c + 1) * MXU_N] = _dot(yb, w_ref[:, lo:lo + MXU_N]).astype(BF16)
    for c in range(q_w // MXU_N):
        lo = q_w + 2 * kv_w + c * MXU_N
        gate_ref[:, c * MXU_N:(c + 1) * MXU_N] = _sigmoid(_dot(yb, w_ref[:, lo:lo + MXU_N])).astype(BF16)


def _attn_inproj(h, gain, w_in, q_gain, k_gain, rope, seq):
    t, d = h.shape
    tm = TM_PROJ
    n_in = w_in.shape[1]
    q_w, kv_w = ATT_HEADS * HEAD_DIM, ATT_KV_HEADS * HEAD_DIM
    seq_tiles = seq // tm
    row = lambda i: (i, 0)
    pos = lambda i: (i % seq_tiles, 0)
    return pl.pallas_call(
        _attn_inproj_kernel,
        grid=(t // tm,),
        in_specs=[
            pl.BlockSpec((tm, d), row),
            _const_spec((1, d)),
            _const_spec((d, n_in)),
            _const_spec((1, HEAD_DIM)),
            _const_spec((1, HEAD_DIM)),
            pl.BlockSpec((tm, HEAD_DIM), pos),
            pl.BlockSpec((tm, HEAD_DIM), pos),
            pl.BlockSpec((tm, HEAD_DIM), pos),
        ],
        out_specs=[
            pl.BlockSpec((tm, q_w), row),
            pl.BlockSpec((tm, kv_w), row),
            pl.BlockSpec((tm, kv_w), row),
            pl.BlockSpec((tm, q_w), row),
        ],
        out_shape=[
            jax.ShapeDtypeStruct((t, q_w), BF16),
            jax.ShapeDtypeStruct((t, kv_w), BF16),
            jax.ShapeDtypeStruct((t, kv_w), BF16),
            jax.ShapeDtypeStruct((t, q_w), BF16),
        ],
        compiler_params=_params("parallel"),
        name="attn_inproj",
    )(h, gain, w_in, q_gain, k_gain, *rope)


def _flash_kernel(q_ref, kt_ref, v_ref, gate_ref, o_ref, m_ref, acc_ref, *, seq, tk):
    hd = HEAD_DIM
    group = q_ref.shape[1] // hd
    tq = q_ref.shape[0]
    q = jnp.concatenate([q_ref[:, g * hd:(g + 1) * hd] for g in range(group)], axis=0)
    m_ref[...] = jnp.full(m_ref.shape, -jnp.inf, F32)
    acc_ref[...] = jnp.zeros(acc_ref.shape, F32)
    ones = jnp.ones((tk, hd), BF16)

    def step(j, carry):
        start = pl.multiple_of(j * tk, tk)
        s = _dot(q, kt_ref[:, pl.ds(start, tk)])
        m_prev = m_ref[...]
        m_new = jnp.maximum(m_prev, jnp.max(s, axis=-1, keepdims=True))
        p = jnp.exp(s - m_new[:, 0:1]).astype(BF16)
        v_ext = jnp.concatenate([v_ref[pl.ds(start, tk), :], ones], axis=1)
        alpha = jnp.exp(m_prev - m_new)
        acc_ref[...] = acc_ref[...] * jnp.concatenate([alpha, alpha], axis=1) + _dot(p, v_ext)
        m_ref[...] = m_new
        return carry

    lax.fori_loop(0, seq // tk, step, 0)
    acc = acc_ref[...]
    o = acc[:, 0:hd] / acc[:, hd:2 * hd]
    for g in range(group):
        o_ref[:, g * hd:(g + 1) * hd] = (
            o[g * tq:(g + 1) * tq] * gate_ref[:, g * hd:(g + 1) * hd].astype(F32)).astype(BF16)


def _flash_attention(q, kt, v, gate, batch, seq):
    t = q.shape[0]
    hd = HEAD_DIM
    group = ATT_HEADS // ATT_KV_HEADS
    tq, tk = ATT_TQ, ATT_TK
    nq = seq // tq
    qmap = lambda b, g, i: (b * nq + i, g)
    return pl.pallas_call(
        functools.partial(_flash_kernel, seq=seq, tk=tk),
        grid=(batch, ATT_KV_HEADS, nq),
        in_specs=[
            pl.BlockSpec((tq, group * hd), qmap),
            pl.BlockSpec((hd, seq), lambda b, g, i: (b * ATT_KV_HEADS + g, 0)),
            pl.BlockSpec((seq, hd), lambda b, g, i: (b, g)),
            pl.BlockSpec((tq, group * hd), qmap),
        ],
        out_specs=pl.BlockSpec((tq, group * hd), qmap),
        out_shape=jax.ShapeDtypeStruct((t, ATT_HEADS * hd), BF16),
        scratch_shapes=[
            pltpu.VMEM((group * tq, hd), F32),
            pltpu.VMEM((group * tq, 2 * hd), F32),
        ],
        compiler_params=_params("parallel", "parallel", "parallel"),
        name="flash_attention",
    )(q, kt, v, gate)


def _outproj_kernel(a_ref, w_ref, res_ref, o_ref):
    o_ref[...] = res_ref[...] + _dot(a_ref[...], w_ref[...])


def _outproj(a, w, res):
    t, k = a.shape
    d = w.shape[1]
    tm = TM_PROJ
    row = lambda i: (i, 0)
    return pl.pallas_call(
        _outproj_kernel,
        grid=(t // tm,),
        in_specs=[pl.BlockSpec((tm, k), row), _const_spec((k, d)), pl.BlockSpec((tm, d), row)],
        out_specs=pl.BlockSpec((tm, d), row),
        out_shape=jax.ShapeDtypeStruct((t, d), F32),
        compiler_params=_params("parallel"),
        name="attn_outproj",
    )(a, w, res)


def _ffn_kernel(x_ref, xp_ref, xn_ref, g_ref, wup_ref, cw_ref, cb_ref, wdn_ref, gf_ref, o_ref, act_ref,
                *, seq_tiles, final_norm):
    tm = x_ref.shape[0]
    d_ff = wdn_ref.shape[0]
    y_ext = _normed_with_halo(x_ref, xp_ref, xn_ref, g_ref[...], seq_tiles)
    for c in range(d_ff // MXU_N):
        ca = slice(c * MXU_N, (c + 1) * MXU_N)
        cb = slice(d_ff + c * MXU_N, d_ff + (c + 1) * MXU_N)
        a = _seq_conv(_dot(y_ext, wup_ref[:, ca]), cw_ref, cb_ref, ca, tm)
        b = _seq_conv(_dot(y_ext, wup_ref[:, cb]), cw_ref, cb_ref, cb, tm)
        act_ref[:, ca] = (a * _sigmoid(a) * b).astype(BF16)
    out = x_ref[...] + _dot(act_ref[...], wdn_ref[...])
    if final_norm:
        out = _rms(out, gf_ref[...])
    o_ref[...] = out


def _ffn(h, gain, w_up, conv_w, conv_b, w_down, final_gain, seq, final_norm):
    t, d = h.shape
    tm = TM_PROJ
    d_ff = w_down.shape[0]
    n_tiles = t // tm
    return pl.pallas_call(
        functools.partial(_ffn_kernel, seq_tiles=seq // tm, final_norm=final_norm),
        grid=(n_tiles,),
        in_specs=_halo_specs(tm, d, n_tiles) + [
            _const_spec((1, d)),
            _const_spec((d, 2 * d_ff)),
            _const_spec(conv_w.shape),
            _const_spec((1, 2 * d_ff)),
            _const_spec((d_ff, d)),
            _const_spec((1, d)),
        ],
        out_specs=pl.BlockSpec((tm, d), lambda i: (i, 0)),
        out_shape=jax.ShapeDtypeStruct((t, d), F32),
        scratch_shapes=[pltpu.VMEM((tm, d_ff), BF16)],
        compiler_params=_params("parallel"),
        name="conv_ffn",
    )(h, h, h, gain, w_up, conv_w, conv_b, w_down, final_gain)


def _gdn_inproj_kernel(x_ref, xp_ref, xn_ref, g_ref, wqkv_ref, wz_ref, wab_ref, cw_ref, cb_ref,
                       alog_ref, dtb_ref, q_ref, k_ref, v_ref, z_ref, gates_ref, *, seq_tiles):
    hd = HEAD_DIM
    tm = x_ref.shape[0]
    qk_w = GDN_QK_HEADS * hd
    y_ext = _normed_with_halo(x_ref, xp_ref, xn_ref, g_ref[...], seq_tiles)
    y = y_ext[0:tm]

    def l2n(z):
        return z * lax.rsqrt(jnp.sum(z * z, axis=-1, keepdims=True) + EPS)

    for c in range(wqkv_ref.shape[1] // MXU_N):
        cols = slice(c * MXU_N, (c + 1) * MXU_N)
        u = _seq_conv(_dot(y_ext, wqkv_ref[:, cols]), cw_ref, cb_ref, cols, tm)
        u = u * _sigmoid(u)
        lo = c * MXU_N
        if lo < qk_w:
            for e in range(MXU_N // hd):
                q_ref[:, lo + e * hd:lo + (e + 1) * hd] = (
                    l2n(u[:, e * hd:(e + 1) * hd]) * (hd ** -0.5)).astype(BF16)
        elif lo < 2 * qk_w:
            for e in range(MXU_N // hd):
                k_ref[:, lo - qk_w + e * hd:lo - qk_w + (e + 1) * hd] = l2n(u[:, e * hd:(e + 1) * hd]).astype(BF16)
        else:
            v_ref[:, lo - 2 * qk_w:lo - 2 * qk_w + MXU_N] = u.astype(BF16)
    for c in range(wz_ref.shape[1] // MXU_N):
        cols = slice(c * MXU_N, (c + 1) * MXU_N)
        z_ref[:, cols] = _dot(y, wz_ref[:, cols]).astype(BF16)

    nh = GDN_V_HEADS
    ab = _dot(y, wab_ref[...])
    lane = lax.broadcasted_iota(jnp.int32, ab.shape, 1)
    xa = ab + dtb_ref[...]
    softplus = jnp.maximum(xa, 0.0) + jnp.log(1.0 + jnp.exp(-jnp.abs(xa)))
    g = -jnp.exp(alog_ref[...]) * softplus
    beta = _sigmoid(ab)
    g1 = g.astype(BF16)
    r1 = g - g1.astype(F32)
    g2 = r1.astype(BF16)
    g3 = (r1 - g2.astype(F32)).astype(BF16)
    gs = jnp.concatenate([g1, g2, g3], axis=1)
    r = lax.broadcasted_iota(jnp.int32, (tm, tm), 0)
    c = lax.broadcasted_iota(jnp.int32, (tm, tm), 1)
    same = (r // GDN_CHUNK) == (c // GDN_CHUNK)
    lower = jnp.where(same & (c <= r), 1.0, 0.0).astype(BF16)
    upper = jnp.where(same & (c >= r), 1.0, 0.0).astype(BF16)
    cf = _dot(lower, gs)
    cb = _dot(upper, gs)
    w = ab.shape[1]
    cum_f = cf[:, 0:w] + cf[:, w:2 * w] + cf[:, 2 * w:3 * w]
    cum_b = cb[:, 0:w] + cb[:, w:2 * w] + cb[:, 2 * w:3 * w]
    gates_ref[...] = jnp.where(lane < nh, cum_f, jnp.where((lane >= 2 * nh) & (lane < 3 * nh), cum_b, beta))


def _gdn_inproj(h, gain, w_qkv, w_z, w_ab, conv_w, conv_b, alog_row, dtb_row, seq):
    t, d = h.shape
    tm = TM_GDN_IN
    hd = HEAD_DIM
    qk_w, v_w = GDN_QK_HEADS * hd, GDN_V_HEADS * hd
    n_tiles = t // tm
    row = lambda i: (i, 0)
    return pl.pallas_call(
        functools.partial(_gdn_inproj_kernel, seq_tiles=seq // tm),
        grid=(n_tiles,),
        in_specs=_halo_specs(tm, d, n_tiles) + [
            _const_spec((1, d)),
            _const_spec(w_qkv.shape),
            _const_spec(w_z.shape),
            _const_spec(w_ab.shape),
            _const_spec(conv_w.shape),
            _const_spec(conv_b.shape),
            _const_spec((1, hd)),
            _const_spec((1, hd)),
        ],
        out_specs=[
            pl.BlockSpec((tm, qk_w), row),
            pl.BlockSpec((tm, qk_w), row),
            pl.BlockSpec((tm, v_w), row),
            pl.BlockSpec((tm, v_w), row),
            pl.BlockSpec((tm, hd), row),
        ],
        out_shape=[
            jax.ShapeDtypeStruct((t, qk_w), BF16),
            jax.ShapeDtypeStruct((t, qk_w), BF16),
            jax.ShapeDtypeStruct((t, v_w), BF16),
            jax.ShapeDtypeStruct((t, v_w), BF16),
            jax.ShapeDtypeStruct((t, hd), F32),
        ],
        compiler_params=_params("parallel"),
        name="gdn_inproj",
    )(h, h, h, gain, w_qkv, w_z, w_ab, conv_w, conv_b, alog_row, dtb_row)


def _unit_triangular_inverse(l_mat, row, col):
    n = l_mat.shape[0]
    size = GDN_DIAG
    same = (row // size) == (col // size)
    ld = jnp.where(same, l_mat, 0.0)
    ldb = ld.astype(BF16)
    q = _dot(ldb, ldb)
    p = jnp.where(row == col, 1.0, 0.0) - ld
    power = 2
    while 2 * power < size:
        qb = q.astype(BF16)
        x = _dot(qb, jnp.concatenate([qb, p.astype(BF16)], axis=1))
        q = x[:, 0:n]
        p = p + x[:, n:2 * n]
        power *= 2
    d = p + _dot(q.astype(BF16), p.astype(BF16))
    while size < n:
        pair = (row // (2 * size)) == (col // (2 * size))
        lo = jnp.where(pair & jnp.logical_not(same), l_mat, 0.0)
        db = d.astype(BF16)
        d = d - _dot(db, _dot(lo.astype(BF16), db).astype(BF16))
        same = pair
        size *= 2
    return d


def _gdn_kernel(q_ref, k_ref, v_ref, gates_ref, gates_t_ref, o_ref, state_ref, *, reverse):
    hd = HEAD_DIM
    n = GDN_CHUNK
    nh = GDN_V_HEADS
    rep = GDN_V_HEADS // GDN_QK_HEADS

    @pl.when(pl.program_id(1) == 0)
    def _():
        state_ref[...] = jnp.zeros(state_ref.shape, F32)

    row = lax.broadcasted_iota(jnp.int32, (n, n), 0)
    col = lax.broadcasted_iota(jnp.int32, (n, n), 1)
    incl = (col >= row) if reverse else (col <= row)
    strict = (col > row) if reverse else (col < row)
    last = 0 if reverse else n - 1
    gc_lane = 2 * nh if reverse else 0
    beta_lane = 3 * nh if reverse else nh
    gates = gates_ref[...]

    for hq in range(GDN_QK_HEADS):
        kq = k_ref[:, hq * hd:(hq + 1) * hd]
        qq = q_ref[:, hq * hd:(hq + 1) * hd]
        kk = _dot_nt(kq, kq)
        qk = _dot_nt(qq, kq)
        kf = kq.astype(F32)
        qf = qq.astype(F32)
        for e in range(rep):
            h = hq * rep + e
            gcol = jnp.broadcast_to(gates[:, gc_lane + h:gc_lane + h + 1], (n, n))
            bcol = jnp.broadcast_to(gates[:, beta_lane + h:beta_lane + h + 1], (n, n))
            grow = gates_t_ref[gc_lane + h:gc_lane + h + 1, :]
            g_last = grow[:, last:last + 1]
            decay = jnp.exp(jnp.where(incl, gcol - grow, -1e30))
            l_mat = jnp.where(strict, kk * decay * bcol, 0.0)
            t_inv = _unit_triangular_inverse(l_mat, row, col)
            eg = jnp.exp(gcol)
            kb = kf * bcol
            vf = v_ref[:, h * hd:(h + 1) * hd].astype(F32)
            rhs = jnp.concatenate([(vf * bcol).astype(BF16), (kb * eg).astype(BF16)], axis=1)
            uw = _dot(t_inv.astype(BF16), rhs)
            u, w = uw[:, 0:hd], uw[:, hd:2 * hd]
            attn = (qk * decay).astype(BF16)
            qd = (qf * eg).astype(BF16)
            kd_t = (kf * jnp.exp(g_last - gcol)).T.astype(BF16)
            state = state_ref[h]
            pq = _dot(jnp.concatenate([w.astype(BF16), qd], axis=0), state.astype(BF16))
            v_new = (u - pq[0:n]).astype(BF16)
            r = _dot(jnp.concatenate([attn, kd_t], axis=0), v_new)
            o_ref[:, h * hd:(h + 1) * hd] = pq[n:2 * n] + r[0:n]
            state_ref[h] = state * jnp.exp(g_last) + r[n:2 * n]


def _gdn_scan(q, k, v, gates, gates_t, batch, seq, reverse):
    t = q.shape[0]
    hd = HEAD_DIM
    n = GDN_CHUNK
    nc = seq // n
    qk_w, v_w = GDN_QK_HEADS * hd, GDN_V_HEADS * hd
    blk = (lambda b, c: b * nc + nc - 1 - c) if reverse else (lambda b, c: b * nc + c)
    row = lambda b, c: (blk(b, c), 0)
    return pl.pallas_call(
        functools.partial(_gdn_kernel, reverse=reverse),
        grid=(batch, nc),
        in_specs=[
            pl.BlockSpec((n, qk_w), row),
            pl.BlockSpec((n, qk_w), row),
            pl.BlockSpec((n, v_w), row),
            pl.BlockSpec((n, hd), row),
            pl.BlockSpec((hd, n), lambda b, c: (0, blk(b, c))),
        ],
        out_specs=pl.BlockSpec((n, v_w), row),
        out_shape=jax.ShapeDtypeStruct((t, v_w), F32),
        scratch_shapes=[pltpu.VMEM((GDN_V_HEADS, hd, hd), F32)],
        compiler_params=_params("parallel", "arbitrary"),
        name="gdn_scan_bwd" if reverse else "gdn_scan_fwd",
    )(q, k, v, gates, gates_t)


def _gdn_outproj_kernel(of_ref, ob_ref, z_ref, g_ref, w_ref, res_ref, o_ref, act_ref):
    hd = HEAD_DIM
    for h in range(GDN_V_HEADS):
        cols = slice(h * hd, (h + 1) * hd)
        o = _rms(of_ref[:, cols] + ob_ref[:, cols], g_ref[...])
        z = z_ref[:, cols].astype(F32)
        act_ref[:, cols] = (o * (z * _sigmoid(z))).astype(BF16)
    o_ref[...] = res_ref[...] + _dot(act_ref[...], w_ref[...])


def _gdn_outproj(o_f, o_b, z, gain, w, res):
    t, k = o_f.shape
    d = w.shape[1]
    tm = TM_PROJ
    row = lambda i: (i, 0)
    return pl.pallas_call(
        _gdn_outproj_kernel,
        grid=(t // tm,),
        in_specs=[
            pl.BlockSpec((tm, k), row),
            pl.BlockSpec((tm, k), row),
            pl.BlockSpec((tm, k), row),
            _const_spec((1, HEAD_DIM)),
            _const_spec((k, d)),
            pl.BlockSpec((tm, d), row),
        ],
        out_specs=pl.BlockSpec((tm, d), row),
        out_shape=jax.ShapeDtypeStruct((t, d), F32),
        scratch_shapes=[pltpu.VMEM((tm, k), BF16)],
        compiler_params=_params("parallel"),
        name="gdn_outproj",
    )(o_f, o_b, z, gain, w, res)


def _rope_tables(seq):
    t = jnp.arange(seq)
    row = (t // ROPE_GRID_W).astype(F32)
    col = (t % ROPE_GRID_W).astype(F32)
    half = HEAD_DIM // 2
    n_freq = half // 2
    inv_freq = ROPE_THETA ** (-(jnp.arange(n_freq, dtype=F32) * 2.0 / half))
    ang_r = row[:, None] * inv_freq[None, :]
    ang_c = col[:, None] * inv_freq[None, :]
    zero = jnp.zeros_like(ang_r)
    cos = jnp.concatenate([jnp.cos(ang_r), jnp.cos(ang_r), jnp.cos(ang_c), jnp.cos(ang_c)], axis=-1)
    sa = jnp.concatenate([-jnp.sin(ang_r), zero, -jnp.sin(ang_c), zero], axis=-1)
    sb = jnp.concatenate([zero, jnp.sin(ang_r), zero, jnp.sin(ang_c)], axis=-1)
    return cos, sa, sb


def kernel(x, norm_mix, norm_ffn, norm_final, attn_w_in, attn_q_norm, attn_k_norm, attn_w_out, gdn_w_in, gdn_conv_w, gdn_conv_b, gdn_a_log, gdn_dt_bias, gdn_o_norm, gdn_w_out, ffn_w_up, ffn_conv_w, ffn_conv_b, ffn_w_down):
    batch, seq, d = x.shape
    t = batch * seq
    hd = HEAD_DIM
    assert seq % TM_PROJ == 0 and seq % ATT_TK == 0 and seq % GDN_CHUNK == 0 and seq % ROPE_GRID_W == 0
    h = x.reshape(t, d)
    row = lambda a: a.reshape(1, -1)
    final_gain = row(norm_final)

    rope = _rope_tables(seq)
    q_gain = row(attn_q_norm[0]) * (hd ** -0.5)
    q, k, v, gate = _attn_inproj(h, row(norm_mix[0]), attn_w_in[0].astype(BF16), q_gain,
                                 row(attn_k_norm[0]), rope, seq)
    kt = k.reshape(batch, seq, ATT_KV_HEADS * hd).transpose(0, 2, 1).reshape(batch * ATT_KV_HEADS * hd, seq)
    attn = _flash_attention(q, kt, v, gate, batch, seq)
    h = _outproj(attn, attn_w_out[0].astype(BF16), h)
    h = _ffn(h, row(norm_ffn[0]), ffn_w_up[0].astype(BF16), ffn_conv_w[0], row(ffn_conv_b[0]),
             ffn_w_down[0].astype(BF16), final_gain, seq, False)

    conv_width = GDN_QK_HEADS * hd * 2 + GDN_V_HEADS * hd
    z_width = GDN_V_HEADS * hd
    w_in = gdn_w_in[0]
    w_qkv = w_in[:, :conv_width].astype(BF16)
    w_z = w_in[:, conv_width:conv_width + z_width].astype(BF16)
    w_ab = jnp.pad(w_in[:, conv_width + z_width:], ((0, 0), (0, hd - 4 * GDN_V_HEADS))).astype(BF16)
    pad_gate = lambda p: jnp.pad(jnp.stack([p[0], jnp.zeros_like(p[0]), p[1], jnp.zeros_like(p[1])]).reshape(1, -1),
                                 ((0, 0), (0, hd - 4 * GDN_V_HEADS)))
    gq, gk, gv, gz, gates = _gdn_inproj(h, row(norm_mix[1]), w_qkv, w_z, w_ab, gdn_conv_w[0], row(gdn_conv_b[0]),
                                        pad_gate(gdn_a_log[0]), pad_gate(gdn_dt_bias[0]), seq)
    gates_t = gates.T
    o_f = _gdn_scan(gq, gk, gv, gates, gates_t, batch, seq, False)
    o_b = _gdn_scan(gq, gk, gv, gates, gates_t, batch, seq, True)
    h = _gdn_outproj(o_f, o_b, gz, row(gdn_o_norm[0]), gdn_w_out[0].astype(BF16), h)
    h = _ffn(h, row(norm_ffn[1]), ffn_w_up[1].astype(BF16), ffn_conv_w[1], row(ffn_conv_b[1]),
             ffn_w_down[1].astype(BF16), final_gain, seq, True)
    return h.reshape(batch, seq, d)
```

```python
import functools

import jax
import jax.numpy as jnp
from jax import lax
from jax.experimental import pallas as pl
from jax.experimental.pallas import tpu as pltpu

F32 = jnp.float32
BF16 = jnp.bfloat16

EPS = 1e-6
ROPE_THETA = 10000.0
ROPE_GRID_W = 64

HEAD_DIM = 128
ATT_HEADS = 8
ATT_KV_HEADS = 2
GDN_QK_HEADS = 8
GDN_V_HEADS = 16
GDN_CHUNK = 128
GDN_DIAG = 16
GDN_HEAD_GROUP = 8

HALO = 8
MXU_N = 256
V7X_VMEM_LIMIT = 56 * 1024 * 1024

TM_PROJ = 512
TM_GDN_IN = 256
ATT_TQ = 256
ATT_TK = 1024


def _dot(a, b):
    return jnp.dot(a, b, preferred_element_type=F32)


def _dot_nt(a, b):
    return lax.dot_general(a, b, (((1,), (1,)), ((), ())), preferred_element_type=F32)


def _rms(x, gain):
    return x * lax.rsqrt(jnp.mean(x * x, axis=-1, keepdims=True) + EPS) * gain


def _sigmoid(x):
    return 1.0 / (1.0 + jnp.exp(-x))


def _const_spec(shape):
    return pl.BlockSpec(shape, lambda *_: (0,) * len(shape), pipeline_mode=pl.Buffered(1))


def _params(*semantics):
    return pltpu.CompilerParams(dimension_semantics=semantics, vmem_limit_bytes=V7X_VMEM_LIMIT)


def _halo_specs(tm, d, n_tiles):
    r = tm // HALO
    last = n_tiles * r - 1
    return [
        pl.BlockSpec((tm, d), lambda i: (i, 0)),
        pl.BlockSpec((HALO, d), lambda i: (jnp.maximum(i * r - 1, 0), 0)),
        pl.BlockSpec((HALO, d), lambda i: (jnp.minimum((i + 1) * r, last), 0)),
    ]


def _normed_with_halo(x_ref, xp_ref, xn_ref, gain, seq_tiles):
    i = pl.program_id(0)
    pos = i % seq_tiles
    y = _rms(x_ref[...], gain)
    yn = jnp.where(pos == seq_tiles - 1, 0.0, _rms(xn_ref[...], gain))
    yp = jnp.where(pos == 0, 0.0, _rms(xp_ref[...], gain))
    return jnp.concatenate([y, yn, yp], axis=0).astype(BF16)


def _seq_conv(u, w_ref, b_ref, cols, tm):
    rows = u.shape[0]
    taps = w_ref.shape[0]
    half = taps // 2
    acc = u * w_ref[half:half + 1, cols]
    for t in range(taps):
        if t == half:
            continue
        shift = (half - t) % rows
        acc = acc + pltpu.roll(u, shift, 0) * w_ref[t:t + 1, cols]
    return acc[0:tm] + b_ref[:, cols]


def _attn_inproj_kernel(x_ref, g_ref, w_ref, qg_ref, kg_ref, cos_ref, sa_ref, sb_ref,
                        q_ref, k_ref, v_ref, gate_ref):
    hd = HEAD_DIM
    yb = _rms(x_ref[...], g_ref[...]).astype(BF16)
    cos, sa, sb = cos_ref[...], sa_ref[...], sb_ref[...]
    q_w = ATT_HEADS * hd
    kv_w = ATT_KV_HEADS * hd

    def norm_rope(z, gain):
        zn = _rms(z, gain)
        return zn * cos + pltpu.roll(zn, hd - hd // 4, 1) * sa + pltpu.roll(zn, hd // 4, 1) * sb

    for c in range(q_w // MXU_N):
        z = _dot(yb, w_ref[:, c * MXU_N:(c + 1) * MXU_N])
        for e in range(MXU_N // hd):
            col = c * MXU_N + e * hd
            q_ref[:, col:col + hd] = norm_rope(z[:, e * hd:(e + 1) * hd], qg_ref[...]).astype(BF16)
    for c in range(kv_w // MXU_N):
        z = _dot(yb, w_ref[:, q_w + c * MXU_N:q_w + (c + 1) * MXU_N])
        for e in range(MXU_N // hd):
            col = c * MXU_N + e * hd
            k_ref[:, col:col + hd] = norm_rope(z[:, e * hd:(e + 1) * hd], kg_ref[...]).astype(BF16)
    for c in range(kv_w // MXU_N):
        lo = q_w + kv_w + c * MXU_N
        v_ref[:, c * MXU_N:(c + 1) * MXU_N] = _dot(yb, w_ref[:, lo:lo + MXU_N]).astype(BF16)
    for c in range(q_w // MXU_N):
        lo = q_w + 2 * kv_w + c * MXU_N
        gate_ref[:, c * MXU_N:(c + 1) * MXU_N] = _sigmoid(_dot(yb, w_ref[:, lo:lo + MXU_N])).astype(BF16)


def _attn_inproj(h, gain, w_in, q_gain, k_gain, rope, seq):
    t, d = h.shape
    tm = TM_PROJ
    n_in = w_in.shape[1]
    q_w, kv_w = ATT_HEADS * HEAD_DIM, ATT_KV_HEADS * HEAD_DIM
    seq_tiles = seq // tm
    row = lambda i: (i, 0)
    pos = lambda i: (i % seq_tiles, 0)
    return pl.pallas_call(
        _attn_inproj_kernel,
        grid=(t // tm,),
        in_specs=[
            pl.BlockSpec((tm, d), row),
            _const_spec((1, d)),
            _const_spec((d, n_in)),
            _const_spec((1, HEAD_DIM)),
            _const_spec((1, HEAD_DIM)),
            pl.BlockSpec((tm, HEAD_DIM), pos),
            pl.BlockSpec((tm, HEAD_DIM), pos),
            pl.BlockSpec((tm, HEAD_DIM), pos),
        ],
        out_specs=[
            pl.BlockSpec((tm, q_w), row),
            pl.BlockSpec((tm, kv_w), row),
            pl.BlockSpec((tm, kv_w), row),
            pl.BlockSpec((tm, q_w), row),
        ],
        out_shape=[
            jax.ShapeDtypeStruct((t, q_w), BF16),
            jax.ShapeDtypeStruct((t, kv_w), BF16),
            jax.ShapeDtypeStruct((t, kv_w), BF16),
            jax.ShapeDtypeStruct((t, q_w), BF16),
        ],
        compiler_params=_params("parallel"),
        name="attn_inproj",
    )(h, gain, w_in, q_gain, k_gain, *rope)


def _flash_kernel(q_ref, kt_ref, v_ref, gate_ref, o_ref, m_ref, acc_ref, *, seq, tk):
    hd = HEAD_DIM
    group = q_ref.shape[1] // hd
    tq = q_ref.shape[0]
    q = jnp.concatenate([q_ref[:, g * hd:(g + 1) * hd] for g in range(group)], axis=0)
    m_ref[...] = jnp.full(m_ref.shape, -jnp.inf, F32)
    acc_ref[...] = jnp.zeros(acc_ref.shape, F32)
    ones = jnp.ones((tk, hd), BF16)

    def step(j, carry):
        start = pl.multiple_of(j * tk, tk)
        s = _dot(q, kt_ref[:, pl.ds(start, tk)])
        m_prev = m_ref[...]
        m_new = jnp.maximum(m_prev, jnp.max(s, axis=-1, keepdims=True))
        p = jnp.exp(s - m_new[:, 0:1]).astype(BF16)
        v_ext = jnp.concatenate([v_ref[pl.ds(start, tk), :], ones], axis=1)
        alpha = jnp.exp(m_prev - m_new)
        acc_ref[...] = acc_ref[...] * jnp.concatenate([alpha, alpha], axis=1) + _dot(p, v_ext)
        m_ref[...] = m_new
        return carry

    lax.fori_loop(0, seq // tk, step, 0)
    acc = acc_ref[...]
    o = acc[:, 0:hd] / acc[:, hd:2 * hd]
    for g in range(group):
        o_ref[:, g * hd:(g + 1) * hd] = (
            o[g * tq:(g + 1) * tq] * gate_ref[:, g * hd:(g + 1) * hd].astype(F32)).astype(BF16)


def _flash_attention(q, kt, v, gate, batch, seq):
    t = q.shape[0]
    hd = HEAD_DIM
    group = ATT_HEADS // ATT_KV_HEADS
    tq, tk = ATT_TQ, ATT_TK
    nq = seq // tq
    qmap = lambda b, g, i: (b * nq + i, g)
    return pl.pallas_call(
        functools.partial(_flash_kernel, seq=seq, tk=tk),
        grid=(batch, ATT_KV_HEADS, nq),
        in_specs=[
            pl.BlockSpec((tq, group * hd), qmap),
            pl.BlockSpec((hd, seq), lambda b, g, i: (b * ATT_KV_HEADS + g, 0)),
            pl.BlockSpec((seq, hd), lambda b, g, i: (b, g)),
            pl.BlockSpec((tq, group * hd), qmap),
        ],
        out_specs=pl.BlockSpec((tq, group * hd), qmap),
        out_shape=jax.ShapeDtypeStruct((t, ATT_HEADS * hd), BF16),
        scratch_shapes=[
            pltpu.VMEM((group * tq, hd), F32),
            pltpu.VMEM((group * tq, 2 * hd), F32),
        ],
        compiler_params=_params("parallel", "parallel", "parallel"),
        name="flash_attention",
    )(q, kt, v, gate)


def _outproj_kernel(a_ref, w_ref, res_ref, o_ref):
    o_ref[...] = res_ref[...] + _dot(a_ref[...], w_ref[...])


def _outproj(a, w, res):
    t, k = a.shape
    d = w.shape[1]
    tm = TM_PROJ
    row = lambda i: (i, 0)
    return pl.pallas_call(
        _outproj_kernel,
        grid=(t // tm,),
        in_specs=[pl.BlockSpec((tm, k), row), _const_spec((k, d)), pl.BlockSpec((tm, d), row)],
        out_specs=pl.BlockSpec((tm, d), row),
        out_shape=jax.ShapeDtypeStruct((t, d), F32),
        compiler_params=_params("parallel"),
        name="attn_outproj",
    )(a, w, res)


def _ffn_kernel(x_ref, xp_ref, xn_ref, g_ref, wup_ref, cw_ref, cb_ref, wdn_ref, gf_ref, o_ref, act_ref,
                *, seq_tiles, final_norm):
    tm = x_ref.shape[0]
    d_ff = wdn_ref.shape[0]
    y_ext = _normed_with_halo(x_ref, xp_ref, xn_ref, g_ref[...], seq_tiles)
    for c in range(d_ff // MXU_N):
        ca = slice(c * MXU_N, (c + 1) * MXU_N)
        cb = slice(d_ff + c * MXU_N, d_ff + (c + 1) * MXU_N)
        a = _seq_conv(_dot(y_ext, wup_ref[:, ca]), cw_ref, cb_ref, ca, tm)
        b = _seq_conv(_dot(y_ext, wup_ref[:, cb]), cw_ref, cb_ref, cb, tm)
        act_ref[:, ca] = (a * _sigmoid(a) * b).astype(BF16)
    out = x_ref[...] + _dot(act_ref[...], wdn_ref[...])
    if final_norm:
        out = _rms(out, gf_ref[...])
    o_ref[...] = out


def _ffn(h, gain, w_up, conv_w, conv_b, w_down, final_gain, seq, final_norm):
    t, d = h.shape
    tm = TM_PROJ
    d_ff = w_down.shape[0]
    n_tiles = t // tm
    return pl.pallas_call(
        functools.partial(_ffn_kernel, seq_tiles=seq // tm, final_norm=final_norm),
        grid=(n_tiles,),
        in_specs=_halo_specs(tm, d, n_tiles) + [
            _const_spec((1, d)),
            _const_spec((d, 2 * d_ff)),
            _const_spec(conv_w.shape),
            _const_spec((1, 2 * d_ff)),
            _const_spec((d_ff, d)),
            _const_spec((1, d)),
        ],
        out_specs=pl.BlockSpec((tm, d), lambda i: (i, 0)),
        out_shape=jax.ShapeDtypeStruct((t, d), F32),
        scratch_shapes=[pltpu.VMEM((tm, d_ff), BF16)],
        compiler_params=_params("parallel"),
        name="conv_ffn",
    )(h, h, h, gain, w_up, conv_w, conv_b, w_down, final_gain)


def _gdn_inproj_kernel(x_ref, xp_ref, xn_ref, g_ref, wqkv_ref, wz_ref, wab_ref, cw_ref, cb_ref,
                       alog_ref, dtb_ref, q_ref, k_ref, v_ref, z_ref, gates_ref, *, seq_tiles):
    hd = HEAD_DIM
    tm = x_ref.shape[0]
    qk_w = GDN_QK_HEADS * hd
    y_ext = _normed_with_halo(x_ref, xp_ref, xn_ref, g_ref[...], seq_tiles)
    y = y_ext[0:tm]

    def l2n(z):
        return z * lax.rsqrt(jnp.sum(z * z, axis=-1, keepdims=True) + EPS)

    for c in range(wqkv_ref.shape[1] // MXU_N):
        cols = slice(c * MXU_N, (c + 1) * MXU_N)
        u = _seq_conv(_dot(y_ext, wqkv_ref[:, cols]), cw_ref, cb_ref, cols, tm)
        u = u * _sigmoid(u)
        lo = c * MXU_N
        if lo < qk_w:
            for e in range(MXU_N // hd):
                q_ref[:, lo + e * hd:lo + (e + 1) * hd] = (
                    l2n(u[:, e * hd:(e + 1) * hd]) * (hd ** -0.5)).astype(BF16)
        elif lo < 2 * qk_w:
            for e in range(MXU_N // hd):
                k_ref[:, lo - qk_w + e * hd:lo - qk_w + (e + 1) * hd] = l2n(u[:, e * hd:(e + 1) * hd]).astype(BF16)
        else:
            v_ref[:, lo - 2 * qk_w:lo - 2 * qk_w + MXU_N] = u.astype(BF16)
    for c in range(wz_ref.shape[1] // MXU_N):
        cols = slice(c * MXU_N, (c + 1) * MXU_N)
        z_ref[:, cols] = _dot(y, wz_ref[:, cols]).astype(BF16)

    nh = GDN_V_HEADS
    ab = _dot(y, wab_ref[...])
    lane = lax.broadcasted_iota(jnp.int32, ab.shape, 1)
    xa = ab + dtb_ref[...]
    softplus = jnp.maximum(xa, 0.0) + jnp.log(1.0 + jnp.exp(-jnp.abs(xa)))
    g = -jnp.exp(alog_ref[...]) * softplus
    beta = _sigmoid(ab)
    g1 = g.astype(BF16)
    r1 = g - g1.astype(F32)
    g2 = r1.astype(BF16)
    g3 = (r1 - g2.astype(F32)).astype(BF16)
    gs = jnp.concatenate([g1, g2, g3], axis=1)
    r = lax.broadcasted_iota(jnp.int32, (tm, tm), 0)
    c = lax.broadcasted_iota(jnp.int32, (tm, tm), 1)
    same = (r // GDN_CHUNK) == (c // GDN_CHUNK)
    lower = jnp.where(same & (c <= r), 1.0, 0.0).astype(BF16)
    upper = jnp.where(same & (c >= r), 1.0, 0.0).astype(BF16)
    cf = _dot(lower, gs)
    cb = _dot(upper, gs)
    w = ab.shape[1]
    cum_f = cf[:, 0:w] + cf[:, w:2 * w] + cf[:, 2 * w:3 * w]
    cum_b = cb[:, 0:w] + cb[:, w:2 * w] + cb[:, 2 * w:3 * w]
    gates_ref[...] = jnp.where(lane < nh, cum_f, jnp.where((lane >= 2 * nh) & (lane < 3 * nh), cum_b, beta))


def _gdn_inproj(h, gain, w_qkv, w_z, w_ab, conv_w, conv_b, alog_row, dtb_row, seq):
    t, d = h.shape
    tm = TM_GDN_IN
    hd = HEAD_DIM
    qk_w, v_w = GDN_QK_HEADS * hd, GDN_V_HEADS * hd
    n_tiles = t // tm
    row = lambda i: (i, 0)
    return pl.pallas_call(
        functools.partial(_gdn_inproj_kernel, seq_tiles=seq // tm),
        grid=(n_tiles,),
        in_specs=_halo_specs(tm, d, n_tiles) + [
            _const_spec((1, d)),
            _const_spec(w_qkv.shape),
            _const_spec(w_z.shape),
            _const_spec(w_ab.shape),
            _const_spec(conv_w.shape),
            _const_spec(conv_b.shape),
            _const_spec((1, hd)),
            _const_spec((1, hd)),
        ],
        out_specs=[
            pl.BlockSpec((tm, qk_w), row),
            pl.BlockSpec((tm, qk_w), row),
            pl.BlockSpec((tm, v_w), row),
            pl.BlockSpec((tm, v_w), row),
            pl.BlockSpec((tm, hd), row),
        ],
        out_shape=[
            jax.ShapeDtypeStruct((t, qk_w), BF16),
            jax.ShapeDtypeStruct((t, qk_w), BF16),
            jax.ShapeDtypeStruct((t, v_w), BF16),
            jax.ShapeDtypeStruct((t, v_w), BF16),
            jax.ShapeDtypeStruct((t, hd), F32),
        ],
        compiler_params=_params("parallel"),
        name="gdn_inproj",
    )(h, h, h, gain, w_qkv, w_z, w_ab, conv_w, conv_b, alog_row, dtb_row)


def _unit_triangular_inverses(l_mats, row, col):
    n = l_mats[0].shape[0]
    size = GDN_DIAG
    same = (row // size) == (col // size)
    eye = jnp.where(row == col, 1.0, 0.0)
    lds = [jnp.where(same, l, 0.0) for l in l_mats]
    ldbs = [ld.astype(BF16) for ld in lds]
    qs = [_dot(ldb, ldb) for ldb in ldbs]
    ps = [eye - ld for ld in lds]
    power = 2
    while 2 * power < size:
        qbs = [q.astype(BF16) for q in qs]
        xs = [_dot(qb, jnp.concatenate([qb, p.astype(BF16)], axis=1)) for qb, p in zip(qbs, ps)]
        qs = [x[:, 0:n] for x in xs]
        ps = [p + x[:, n:2 * n] for p, x in zip(ps, xs)]
        power *= 2
    ds = [p + _dot(q.astype(BF16), p.astype(BF16)) for q, p in zip(qs, ps)]
    while size < n:
        pair = (row // (2 * size)) == (col // (2 * size))
        join = pair & jnp.logical_not(same)
        dbs = [d.astype(BF16) for d in ds]
        es = [_dot(jnp.where(join, l, 0.0).astype(BF16), db).astype(BF16) for l, db in zip(l_mats, dbs)]
        ds = [d - _dot(db, e) for d, db, e in zip(ds, dbs, es)]
        same = pair
        size *= 2
    return ds


def _gdn_kernel(q_ref, k_ref, v_ref, gates_ref, gates_t_ref, o_ref, state_ref, *, reverse):
    hd = HEAD_DIM
    n = GDN_CHUNK
    nh = GDN_V_HEADS
    rep = GDN_V_HEADS // GDN_QK_HEADS

    @pl.when(pl.program_id(1) == 0)
    def _():
        state_ref[...] = jnp.zeros(state_ref.shape, F32)

    row = lax.broadcasted_iota(jnp.int32, (n, n), 0)
    col = lax.broadcasted_iota(jnp.int32, (n, n), 1)
    incl = (col >= row) if reverse else (col <= row)
    strict = (col > row) if reverse else (col < row)
    last = 0 if reverse else n - 1
    gc_lane = 2 * nh if reverse else 0
    beta_lane = 3 * nh if reverse else nh
    gates = gates_ref[...]

    for first in range(0, nh, GDN_HEAD_GROUP):
        heads = list(range(first, first + GDN_HEAD_GROUP))
        qk_heads = sorted({h // rep for h in heads})
        kq = {hq: k_ref[:, hq * hd:(hq + 1) * hd] for hq in qk_heads}
        qq = {hq: q_ref[:, hq * hd:(hq + 1) * hd] for hq in qk_heads}
        kk = {hq: _dot_nt(kq[hq], kq[hq]) for hq in qk_heads}
        qk = {hq: _dot_nt(qq[hq], kq[hq]) for hq in qk_heads}
        gcol, bcol, g_last, decay, l_mats = [], [], [], [], []
        for h in heads:
            gc = jnp.broadcast_to(gates[:, gc_lane + h:gc_lane + h + 1], (n, n))
            bc = jnp.broadcast_to(gates[:, beta_lane + h:beta_lane + h + 1], (n, n))
            grow = gates_t_ref[gc_lane + h:gc_lane + h + 1, :]
            dec = jnp.exp(jnp.where(incl, gc - grow, -1e30))
            gcol.append(gc)
            bcol.append(bc)
            g_last.append(grow[:, last:last + 1])
            decay.append(dec)
            l_mats.append(jnp.where(strict, kk[h // rep] * dec * bc, 0.0))
        t_inv = _unit_triangular_inverses(l_mats, row, col)
        uw, lhs_state, lhs_new = [], [], []
        for i, h in enumerate(heads):
            kf = kq[h // rep].astype(F32)
            qf = qq[h // rep].astype(F32)
            eg = jnp.exp(gcol[i])
            vf = v_ref[:, h * hd:(h + 1) * hd].astype(F32)
            rhs = jnp.concatenate([(vf * bcol[i]).astype(BF16), (kf * bcol[i] * eg).astype(BF16)], axis=1)
            uw.append(_dot(t_inv[i].astype(BF16), rhs))
            attn = (qk[h // rep] * decay[i]).astype(BF16)
            kd_t = (kf * jnp.exp(g_last[i] - gcol[i])).T.astype(BF16)
            lhs_state.append((qf * eg).astype(BF16))
            lhs_new.append(jnp.concatenate([attn, kd_t], axis=0))
        states = [state_ref[h] for h in heads]
        pq = [_dot(jnp.concatenate([x[:, hd:2 * hd].astype(BF16), qd], axis=0), s.astype(BF16))
              for x, qd, s in zip(uw, lhs_state, states)]
        v_new = [(x[:, 0:hd] - y[0:n]).astype(BF16) for x, y in zip(uw, pq)]
        r = [_dot(a, b) for a, b in zip(lhs_new, v_new)]
        for i, h in enumerate(heads):
            o_ref[:, h * hd:(h + 1) * hd] = pq[i][n:2 * n] + r[i][0:n]
            state_ref[h] = states[i] * jnp.exp(g_last[i]) + r[i][n:2 * n]


def _gdn_scan(q, k, v, gates, gates_t, batch, seq, reverse):
    t = q.shape[0]
    hd = HEAD_DIM
    n = GDN_CHUNK
    nc = seq // n
    qk_w, v_w = GDN_QK_HEADS * hd, GDN_V_HEADS * hd
    blk = (lambda b, c: b * nc + nc - 1 - c) if reverse else (lambda b, c: b * nc + c)
    row = lambda b, c: (blk(b, c), 0)
    return pl.pallas_call(
        functools.partial(_gdn_kernel, reverse=reverse),
        grid=(batch, nc),
        in_specs=[
            pl.BlockSpec((n, qk_w), row),
            pl.BlockSpec((n, qk_w), row),
            pl.BlockSpec((n, v_w), row),
            pl.BlockSpec((n, hd), row),
            pl.BlockSpec((hd, n), lambda b, c: (0, blk(b, c))),
        ],
        out_specs=pl.BlockSpec((n, v_w), row),
        out_shape=jax.ShapeDtypeStruct((t, v_w), F32),
        scratch_shapes=[pltpu.VMEM((GDN_V_HEADS, hd, hd), F32)],
        compiler_params=_params("parallel", "arbitrary"),
        name="gdn_scan_bwd" if reverse else "gdn_scan_fwd",
    )(q, k, v, gates, gates_t)


def _gdn_outproj_kernel(of_ref, ob_ref, z_ref, g_ref, w_ref, res_ref, o_ref, act_ref):
    hd = HEAD_DIM
    for h in range(GDN_V_HEADS):
        cols = slice(h * hd, (h + 1) * hd)
        o = _rms(of_ref[:, cols] + ob_ref[:, cols], g_ref[...])
        z = z_ref[:, cols].astype(F32)
        act_ref[:, cols] = (o * (z * _sigmoid(z))).astype(BF16)
    o_ref[...] = res_ref[...] + _dot(act_ref[...], w_ref[...])


def _gdn_outproj(o_f, o_b, z, gain, w, res):
    t, k = o_f.shape
    d = w.shape[1]
    tm = TM_PROJ
    row = lambda i: (i, 0)
    return pl.pallas_call(
        _gdn_outproj_kernel,
        grid=(t // tm,),
        in_specs=[
            pl.BlockSpec((tm, k), row),
            pl.BlockSpec((tm, k), row),
            pl.BlockSpec((tm, k), row),
            _const_spec((1, HEAD_DIM)),
            _const_spec((k, d)),
            pl.BlockSpec((tm, d), row),
        ],
        out_specs=pl.BlockSpec((tm, d), row),
        out_shape=jax.ShapeDtypeStruct((t, d), F32),
        scratch_shapes=[pltpu.VMEM((tm, k), BF16)],
        compiler_params=_params("parallel"),
        name="gdn_outproj",
    )(o_f, o_b, z, gain, w, res)


def _rope_tables(seq):
    t = jnp.arange(seq)
    row = (t // ROPE_GRID_W).astype(F32)
    col = (t % ROPE_GRID_W).astype(F32)
    half = HEAD_DIM // 2
    n_freq = half // 2
    inv_freq = ROPE_THETA ** (-(jnp.arange(n_freq, dtype=F32) * 2.0 / half))
    ang_r = row[:, None] * inv_freq[None, :]
    ang_c = col[:, None] * inv_freq[None, :]
    zero = jnp.zeros_like(ang_r)
    cos = jnp.concatenate([jnp.cos(ang_r), jnp.cos(ang_r), jnp.cos(ang_c), jnp.cos(ang_c)], axis=-1)
    sa = jnp.concatenate([-jnp.sin(ang_r), zero, -jnp.sin(ang_c), zero], axis=-1)
    sb = jnp.concatenate([zero, jnp.sin(ang_r), zero, jnp.sin(ang_c)], axis=-1)
    return cos, sa, sb


def kernel(x, norm_mix, norm_ffn, norm_final, attn_w_in, attn_q_norm, attn_k_norm, attn_w_out, gdn_w_in, gdn_conv_w, gdn_conv_b, gdn_a_log, gdn_dt_bias, gdn_o_norm, gdn_w_out, ffn_w_up, ffn_conv_w, ffn_conv_b, ffn_w_down):
    batch, seq, d = x.shape
    t = batch * seq
    hd = HEAD_DIM
    assert seq % TM_PROJ == 0 and seq % ATT_TK == 0 and seq % GDN_CHUNK == 0 and seq % ROPE_GRID_W == 0
    h = x.reshape(t, d)
    row = lambda a: a.reshape(1, -1)
    final_gain = row(norm_final)

    rope = _rope_tables(seq)
    q_gain = row(attn_q_norm[0]) * (hd ** -0.5)
    q, k, v, gate = _attn_inproj(h, row(norm_mix[0]), attn_w_in[0].astype(BF16), q_gain,
                                 row(attn_k_norm[0]), rope, seq)
    kt = k.reshape(batch, seq, ATT_KV_HEADS * hd).transpose(0, 2, 1).reshape(batch * ATT_KV_HEADS * hd, seq)
    attn = _flash_attention(q, kt, v, gate, batch, seq)
    h = _outproj(attn, attn_w_out[0].astype(BF16), h)
    h = _ffn(h, row(norm_ffn[0]), ffn_w_up[0].astype(BF16), ffn_conv_w[0], row(ffn_conv_b[0]),
             ffn_w_down[0].astype(BF16), final_gain, seq, False)

    conv_width = GDN_QK_HEADS * hd * 2 + GDN_V_HEADS * hd
    z_width = GDN_V_HEADS * hd
    w_in = gdn_w_in[0]
    w_qkv = w_in[:, :conv_width].astype(BF16)
    w_z = w_in[:, conv_width:conv_width + z_width].astype(BF16)
    w_ab = jnp.pad(w_in[:, conv_width + z_width:], ((0, 0), (0, hd - 4 * GDN_V_HEADS))).astype(BF16)
    pad_gate = lambda p: jnp.pad(jnp.stack([p[0], jnp.zeros_like(p[0]), p[1], jnp.zeros_like(p[1])]).reshape(1, -1),
                                 ((0, 0), (0, hd - 4 * GDN_V_HEADS)))
    gq, gk, gv, gz, gates = _gdn_inproj(h, row(norm_mix[1]), w_qkv, w_z, w_ab, gdn_conv_w[0], row(gdn_conv_b[0]),
                                        pad_gate(gdn_a_log[0]), pad_gate(gdn_dt_bias[0]), seq)
    gates_t = gates.T
    o_f = _gdn_scan(gq, gk, gv, gates, gates_t, batch, seq, False)
    o_b = _gdn_scan(gq, gk, gv, gates, gates_t, batch, seq, True)
    h = _gdn_outproj(o_f, o_b, gz, row(gdn_o_norm[0]), gdn_w_out[0].astype(BF16), h)
    h = _ffn(h, row(norm_ffn[1]), ffn_w_up[1].astype(BF16), ffn_conv_w[1], row(ffn_conv_b[1]),
             ffn_w_down[1].astype(BF16), final_gain, seq, True)
    return h.reshape(batch, seq, d)
```

```python
import functools

import jax
import jax.numpy as jnp
from jax import lax
from jax.experimental import pallas as pl
from jax.experimental.pallas import tpu as pltpu

F32 = jnp.float32
BF16 = jnp.bfloat16

EPS = 1e-6
ROPE_THETA = 10000.0
ROPE_GRID_W = 64

HEAD_DIM = 128
ATT_HEADS = 8
ATT_KV_HEADS = 2
GDN_QK_HEADS = 8
GDN_V_HEADS = 16
GDN_CHUNK = 128
GDN_DIAG = 16
GDN_HEAD_GROUP = 8

HALO = 8
MXU_N = 256
V7X_VMEM_LIMIT = 56 * 1024 * 1024

TM_PROJ = 512
TM_GDN_IN = 256
ATT_TQ = 256
ATT_TK = 512
ATT_KV_UNROLL = 8
ATT_LOOKAHEAD = 4
ATT_ONES_ROWS = 16
LOG2E = 1.4426950408889634


def _dot(a, b):
    return jnp.dot(a, b, preferred_element_type=F32)


def _dot_nt(a, b):
    return lax.dot_general(a, b, (((1,), (1,)), ((), ())), preferred_element_type=F32)


def _rms(x, gain):
    return x * lax.rsqrt(jnp.mean(x * x, axis=-1, keepdims=True) + EPS) * gain


def _sigmoid(x):
    return 1.0 / (1.0 + jnp.exp(-x))


def _const_spec(shape):
    return pl.BlockSpec(shape, lambda *_: (0,) * len(shape), pipeline_mode=pl.Buffered(1))


def _params(*semantics):
    return pltpu.CompilerParams(dimension_semantics=semantics, vmem_limit_bytes=V7X_VMEM_LIMIT)


def _halo_specs(tm, d, n_tiles):
    r = tm // HALO
    last = n_tiles * r - 1
    return [
        pl.BlockSpec((tm, d), lambda i: (i, 0)),
        pl.BlockSpec((HALO, d), lambda i: (jnp.maximum(i * r - 1, 0), 0)),
        pl.BlockSpec((HALO, d), lambda i: (jnp.minimum((i + 1) * r, last), 0)),
    ]


def _normed_with_halo(x_ref, xp_ref, xn_ref, gain, seq_tiles):
    i = pl.program_id(0)
    pos = i % seq_tiles
    y = _rms(x_ref[...], gain)
    yn = jnp.where(pos == seq_tiles - 1, 0.0, _rms(xn_ref[...], gain))
    yp = jnp.where(pos == 0, 0.0, _rms(xp_ref[...], gain))
    return jnp.concatenate([y, yn, yp], axis=0).astype(BF16)


def _shift_rows(u, k):
    rows, cols = u.shape
    nb = rows // HALO
    blocks = u.reshape(nb, HALO, cols)
    sub = lax.broadcasted_iota(jnp.int32, (1, HALO, cols), 1)
    if k > 0:
        mixed = jnp.where(sub >= HALO - k, jnp.concatenate([blocks[nb - 1:], blocks[:nb - 1]], axis=0), blocks)
    else:
        mixed = jnp.where(sub < -k, jnp.concatenate([blocks[1:], blocks[:1]], axis=0), blocks)
    return pltpu.roll(mixed, k % HALO, axis=1).reshape(rows, cols)


def _seq_conv(u, w_ref, b_ref, cols, tm):
    taps = w_ref.shape[0]
    half = taps // 2
    acc = u * w_ref[half:half + 1, cols]
    for t in range(taps):
        if t != half:
            acc = acc + _shift_rows(u, half - t) * w_ref[t:t + 1, cols]
    return acc[0:tm] + b_ref[:, cols]


def _attn_inproj_kernel(x_ref, g_ref, w_ref, qg_ref, kg_ref, cos_ref, sa_ref, sb_ref,
                        q_ref, k_ref, v_ref, gate_ref):
    hd = HEAD_DIM
    yb = _rms(x_ref[...], g_ref[...]).astype(BF16)
    cos, sa, sb = cos_ref[...], sa_ref[...], sb_ref[...]
    q_w = ATT_HEADS * hd
    kv_w = ATT_KV_HEADS * hd

    def norm_rope(z, gain):
        zn = _rms(z, gain)
        return zn * cos + pltpu.roll(zn, hd - hd // 4, 1) * sa + pltpu.roll(zn, hd // 4, 1) * sb

    for c in range(q_w // MXU_N):
        z = _dot(yb, w_ref[:, c * MXU_N:(c + 1) * MXU_N])
        for e in range(MXU_N // hd):
            col = c * MXU_N + e * hd
            q_ref[:, col:col + hd] = norm_rope(z[:, e * hd:(e + 1) * hd], qg_ref[...]).astype(BF16)
    for c in range(kv_w // MXU_N):
        z = _dot(yb, w_ref[:, q_w + c * MXU_N:q_w + (c + 1) * MXU_N])
        for e in range(MXU_N // hd):
            col = c * MXU_N + e * hd
            k_ref[:, col:col + hd] = norm_rope(z[:, e * hd:(e + 1) * hd], kg_ref[...]).astype(BF16)
    for c in range(kv_w // MXU_N):
        lo = q_w + kv_w + c * MXU_N
        v_ref[:, c * MXU_N:(c + 1) * MXU_N] = _dot(yb, w_ref[:, lo:lo + MXU_N]).astype(BF16)
    for c in range(q_w // MXU_N):
        lo = q_w + 2 * kv_w + c * MXU_N
        gate_ref[:, c * MXU_N:(c + 1) * MXU_N] = _sigmoid(_dot(yb, w_ref[:, lo:lo + MXU_N])).astype(BF16)


def _attn_inproj(h, gain, w_in, q_gain, k_gain, rope, seq):
    t, d = h.shape
    tm = TM_PROJ
    n_in = w_in.shape[1]
    q_w, kv_w = ATT_HEADS * HEAD_DIM, ATT_KV_HEADS * HEAD_DIM
    seq_tiles = seq // tm
    row = lambda i: (i, 0)
    pos = lambda i: (i % seq_tiles, 0)
    return pl.pallas_call(
        _attn_inproj_kernel,
        grid=(t // tm,),
        in_specs=[
            pl.BlockSpec((tm, d), row),
            _const_spec((1, d)),
            _const_spec((d, n_in)),
            _const_spec((1, HEAD_DIM)),
            _const_spec((1, HEAD_DIM)),
            pl.BlockSpec((tm, HEAD_DIM), pos),
            pl.BlockSpec((tm, HEAD_DIM), pos),
            pl.BlockSpec((tm, HEAD_DIM), pos),
        ],
        out_specs=[
            pl.BlockSpec((tm, q_w), row),
            pl.BlockSpec((tm, kv_w), row),
            pl.BlockSpec((tm, kv_w), row),
            pl.BlockSpec((tm, q_w), row),
        ],
        out_shape=[
            jax.ShapeDtypeStruct((t, q_w), BF16),
            jax.ShapeDtypeStruct((t, kv_w), BF16),
            jax.ShapeDtypeStruct((t, kv_w), BF16),
            jax.ShapeDtypeStruct((t, q_w), BF16),
        ],
        compiler_params=_params("parallel"),
        name="attn_inproj",
    )(h, gain, w_in, q_gain, k_gain, *rope)


def _flash_kernel(q_ref, k_ref, vt_ref, gate_ref, o_ref, qt_ref, m_ref, acc_ref, *, seq):
    hd = HEAD_DIM
    tk = ATT_TK
    group = q_ref.shape[1] // hd
    for g in range(group):
        qt_ref[g] = q_ref[:, g * hd:(g + 1) * hd].astype(F32).T.astype(BF16)
    m_ref[...] = jnp.full(m_ref.shape, -jnp.inf, F32)
    acc_ref[...] = jnp.zeros(acc_ref.shape, F32)

    def scores(unit):
        start, g = unit
        return _dot(k_ref[pl.ds(start, tk), :], qt_ref[g])

    def step(j, carry):
        units = []
        for u in range(ATT_KV_UNROLL):
            start = pl.multiple_of((j * ATT_KV_UNROLL + u) * tk, tk)
            units += [(start, g) for g in range(group)]
        pending = [scores(unit) for unit in units[:ATT_LOOKAHEAD]]
        for idx, (start, g) in enumerate(units):
            s = pending.pop(0)
            if idx + ATT_LOOKAHEAD < len(units):
                pending.append(scores(units[idx + ATT_LOOKAHEAD]))
            m_prev = m_ref[g]
            m_new = jnp.maximum(m_prev, jnp.max(s, axis=0, keepdims=True))
            p = jnp.exp2(s - m_new).astype(BF16)
            acc_ref[g] = acc_ref[g] * jnp.exp2(m_prev - m_new) + _dot(vt_ref[:, pl.ds(start, tk)], p)
            m_ref[g] = m_new
        return carry

    lax.fori_loop(0, seq // (tk * ATT_KV_UNROLL), step, 0)
    for g in range(group):
        acc = acc_ref[g]
        o = (acc[0:hd] / acc[hd:hd + 1]).T
        o_ref[:, g * hd:(g + 1) * hd] = (o * gate_ref[:, g * hd:(g + 1) * hd].astype(F32)).astype(BF16)


def _flash_attention(q, k, vt, gate, batch, seq):
    t = q.shape[0]
    hd = HEAD_DIM
    group = ATT_HEADS // ATT_KV_HEADS
    tq = ATT_TQ
    nq = seq // tq
    rows = vt.shape[1]
    qmap = lambda b, g, i: (b * nq + i, g)
    return pl.pallas_call(
        functools.partial(_flash_kernel, seq=seq),
        grid=(batch, ATT_KV_HEADS, nq),
        in_specs=[
            pl.BlockSpec((tq, group * hd), qmap),
            pl.BlockSpec((seq, hd), lambda b, g, i: (b, g)),
            pl.BlockSpec((None, rows, seq), lambda b, g, i: (b * ATT_KV_HEADS + g, 0, 0)),
            pl.BlockSpec((tq, group * hd), qmap),
        ],
        out_specs=pl.BlockSpec((tq, group * hd), qmap),
        out_shape=jax.ShapeDtypeStruct((t, ATT_HEADS * hd), BF16),
        scratch_shapes=[
            pltpu.VMEM((group, hd, tq), BF16),
            pltpu.VMEM((group, 1, tq), F32),
            pltpu.VMEM((group, rows, tq), F32),
        ],
        compiler_params=_params("parallel", "parallel", "parallel"),
        name="flash_attention",
    )(q, k, vt, gate)


def _outproj_kernel(a_ref, w_ref, res_ref, o_ref):
    o_ref[...] = res_ref[...] + _dot(a_ref[...], w_ref[...])


def _outproj(a, w, res):
    t, k = a.shape
    d = w.shape[1]
    tm = TM_PROJ
    row = lambda i: (i, 0)
    return pl.pallas_call(
        _outproj_kernel,
        grid=(t // tm,),
        in_specs=[pl.BlockSpec((tm, k), row), _const_spec((k, d)), pl.BlockSpec((tm, d), row)],
        out_specs=pl.BlockSpec((tm, d), row),
        out_shape=jax.ShapeDtypeStruct((t, d), F32),
        compiler_params=_params("parallel"),
        name="attn_outproj",
    )(a, w, res)


def _ffn_kernel(x_ref, xp_ref, xn_ref, g_ref, wup_ref, cw_ref, cb_ref, wdn_ref, gf_ref, o_ref, act_ref,
                *, seq_tiles, final_norm):
    tm = x_ref.shape[0]
    d_ff = wdn_ref.shape[0]
    y_ext = _normed_with_halo(x_ref, xp_ref, xn_ref, g_ref[...], seq_tiles)
    for c in range(d_ff // MXU_N):
        ca = slice(c * MXU_N, (c + 1) * MXU_N)
        cb = slice(d_ff + c * MXU_N, d_ff + (c + 1) * MXU_N)
        a = _seq_conv(_dot(y_ext, wup_ref[:, ca]), cw_ref, cb_ref, ca, tm)
        b = _seq_conv(_dot(y_ext, wup_ref[:, cb]), cw_ref, cb_ref, cb, tm)
        act_ref[:, ca] = (a * _sigmoid(a) * b).astype(BF16)
    out = x_ref[...] + _dot(act_ref[...], wdn_ref[...])
    if final_norm:
        out = _rms(out, gf_ref[...])
    o_ref[...] = out


def _ffn(h, gain, w_up, conv_w, conv_b, w_down, final_gain, seq, final_norm):
    t, d = h.shape
    tm = TM_PROJ
    d_ff = w_down.shape[0]
    n_tiles = t // tm
    return pl.pallas_call(
        functools.partial(_ffn_kernel, seq_tiles=seq // tm, final_norm=final_norm),
        grid=(n_tiles,),
        in_specs=_halo_specs(tm, d, n_tiles) + [
            _const_spec((1, d)),
            _const_spec((d, 2 * d_ff)),
            _const_spec(conv_w.shape),
            _const_spec((1, 2 * d_ff)),
            _const_spec((d_ff, d)),
            _const_spec((1, d)),
        ],
        out_specs=pl.BlockSpec((tm, d), lambda i: (i, 0)),
        out_shape=jax.ShapeDtypeStruct((t, d), F32),
        scratch_shapes=[pltpu.VMEM((tm, d_ff), BF16)],
        compiler_params=_params("parallel"),
        name="conv_ffn",
    )(h, h, h, gain, w_up, conv_w, conv_b, w_down, final_gain)


def _gdn_inproj_kernel(x_ref, xp_ref, xn_ref, g_ref, wqkv_ref, wz_ref, wab_ref, cw_ref, cb_ref,
                       alog_ref, dtb_ref, q_ref, k_ref, v_ref, z_ref, gates_ref, *, seq_tiles):
    hd = HEAD_DIM
    tm = x_ref.shape[0]
    qk_w = GDN_QK_HEADS * hd
    y_ext = _normed_with_halo(x_ref, xp_ref, xn_ref, g_ref[...], seq_tiles)
    y = y_ext[0:tm]

    def l2n(z, scale=1.0):
        return z * (lax.rsqrt(jnp.sum(z * z, axis=-1, keepdims=True) + EPS) * scale)

    for c in range(wqkv_ref.shape[1] // MXU_N):
        cols = slice(c * MXU_N, (c + 1) * MXU_N)
        u = _seq_conv(_dot(y_ext, wqkv_ref[:, cols]), cw_ref, cb_ref, cols, tm)
        u = u * _sigmoid(u)
        lo = c * MXU_N
        if lo < qk_w:
            for e in range(MXU_N // hd):
                q_ref[:, lo + e * hd:lo + (e + 1) * hd] = l2n(u[:, e * hd:(e + 1) * hd], hd ** -0.5).astype(BF16)
        elif lo < 2 * qk_w:
            for e in range(MXU_N // hd):
                k_ref[:, lo - qk_w + e * hd:lo - qk_w + (e + 1) * hd] = l2n(u[:, e * hd:(e + 1) * hd]).astype(BF16)
        else:
            v_ref[:, lo - 2 * qk_w:lo - 2 * qk_w + MXU_N] = u.astype(BF16)
    for c in range(wz_ref.shape[1] // MXU_N):
        cols = slice(c * MXU_N, (c + 1) * MXU_N)
        z_ref[:, cols] = _dot(y, wz_ref[:, cols]).astype(BF16)

    nh = GDN_V_HEADS
    ab = _dot(y, wab_ref[...])
    lane = lax.broadcasted_iota(jnp.int32, ab.shape, 1)
    xa = ab + dtb_ref[...]
    softplus = jnp.maximum(xa, 0.0) + jnp.log(1.0 + jnp.exp(-jnp.abs(xa)))
    g = -jnp.exp(alog_ref[...]) * softplus
    beta = _sigmoid(ab)
    g1 = g.astype(BF16)
    r1 = g - g1.astype(F32)
    g2 = r1.astype(BF16)
    g3 = (r1 - g2.astype(F32)).astype(BF16)
    gs = jnp.concatenate([g1, g2, g3], axis=1)
    r = lax.broadcasted_iota(jnp.int32, (tm, tm), 0)
    c = lax.broadcasted_iota(jnp.int32, (tm, tm), 1)
    same = (r // GDN_CHUNK) == (c // GDN_CHUNK)
    lower = jnp.where(same & (c <= r), 1.0, 0.0).astype(BF16)
    upper = jnp.where(same & (c >= r), 1.0, 0.0).astype(BF16)
    cf = _dot(lower, gs)
    cb = _dot(upper, gs)
    w = ab.shape[1]
    cum_f = cf[:, 0:w] + cf[:, w:2 * w] + cf[:, 2 * w:3 * w]
    cum_b = cb[:, 0:w] + cb[:, w:2 * w] + cb[:, 2 * w:3 * w]
    gates_ref[...] = jnp.where(lane < nh, cum_f, jnp.where((lane >= 2 * nh) & (lane < 3 * nh), cum_b, beta))


def _gdn_inproj(h, gain, w_qkv, w_z, w_ab, conv_w, conv_b, alog_row, dtb_row, seq):
    t, d = h.shape
    tm = TM_GDN_IN
    hd = HEAD_DIM
    qk_w, v_w = GDN_QK_HEADS * hd, GDN_V_HEADS * hd
    n_tiles = t // tm
    row = lambda i: (i, 0)
    return pl.pallas_call(
        functools.partial(_gdn_inproj_kernel, seq_tiles=seq // tm),
        grid=(n_tiles,),
        in_specs=_halo_specs(tm, d, n_tiles) + [
            _const_spec((1, d)),
            _const_spec(w_qkv.shape),
            _const_spec(w_z.shape),
            _const_spec(w_ab.shape),
            _const_spec(conv_w.shape),
            _const_spec(conv_b.shape),
            _const_spec((1, hd)),
            _const_spec((1, hd)),
        ],
        out_specs=[
            pl.BlockSpec((tm, qk_w), row),
            pl.BlockSpec((tm, qk_w), row),
            pl.BlockSpec((tm, v_w), row),
            pl.BlockSpec((tm, v_w), row),
            pl.BlockSpec((tm, hd), row),
        ],
        out_shape=[
            jax.ShapeDtypeStruct((t, qk_w), BF16),
            jax.ShapeDtypeStruct((t, qk_w), BF16),
            jax.ShapeDtypeStruct((t, v_w), BF16),
            jax.ShapeDtypeStruct((t, v_w), BF16),
            jax.ShapeDtypeStruct((t, hd), F32),
        ],
        compiler_params=_params("parallel"),
        name="gdn_inproj",
    )(h, h, h, gain, w_qkv, w_z, w_ab, conv_w, conv_b, alog_row, dtb_row)


def _unit_triangular_inverses(l_mats, row, col):
    n = l_mats[0].shape[0]
    size = GDN_DIAG
    same = (row // size) == (col // size)
    eye = jnp.where(row == col, 1.0, 0.0)
    lds = [jnp.where(same, l, 0.0) for l in l_mats]
    ldbs = [ld.astype(BF16) for ld in lds]
    qs = [_dot(ldb, ldb) for ldb in ldbs]
    ps = [eye - ld for ld in lds]
    power = 2
    while 2 * power < size:
        qbs = [q.astype(BF16) for q in qs]
        xs = [_dot(qb, jnp.concatenate([qb, p.astype(BF16)], axis=1)) for qb, p in zip(qbs, ps)]
        qs = [x[:, 0:n] for x in xs]
        ps = [p + x[:, n:2 * n] for p, x in zip(ps, xs)]
        power *= 2
    ds = [p + _dot(q.astype(BF16), p.astype(BF16)) for q, p in zip(qs, ps)]
    while size < n:
        pair = (row // (2 * size)) == (col // (2 * size))
        join = pair & jnp.logical_not(same)
        dbs = [d.astype(BF16) for d in ds]
        es = [_dot(jnp.where(join, l, 0.0).astype(BF16), db).astype(BF16) for l, db in zip(l_mats, dbs)]
        ds = [d - _dot(db, e) for d, db, e in zip(ds, dbs, es)]
        same = pair
        size *= 2
    return ds


def _gdn_kernel(q_ref, k_ref, v_ref, gates_ref, gates_t_ref, o_ref, state_ref, *, reverse):
    hd = HEAD_DIM
    n = GDN_CHUNK
    nh = GDN_V_HEADS
    rep = GDN_V_HEADS // GDN_QK_HEADS

    @pl.when(pl.program_id(1) == 0)
    def _():
        state_ref[...] = jnp.zeros(state_ref.shape, F32)

    row = lax.broadcasted_iota(jnp.int32, (n, n), 0)
    col = lax.broadcasted_iota(jnp.int32, (n, n), 1)
    incl = (col >= row) if reverse else (col <= row)
    strict = (col > row) if reverse else (col < row)
    last = 0 if reverse else n - 1
    gc_lane = 2 * nh if reverse else 0
    beta_lane = 3 * nh if reverse else nh
    gates = gates_ref[...]

    for first in range(0, nh, GDN_HEAD_GROUP):
        heads = list(range(first, first + GDN_HEAD_GROUP))
        qk_heads = sorted({h // rep for h in heads})
        kq = {hq: k_ref[:, hq * hd:(hq + 1) * hd] for hq in qk_heads}
        qq = {hq: q_ref[:, hq * hd:(hq + 1) * hd] for hq in qk_heads}
        kk = {hq: _dot_nt(kq[hq], kq[hq]) for hq in qk_heads}
        qk = {hq: _dot_nt(qq[hq], kq[hq]) for hq in qk_heads}
        gcol, bcol, g_last, decay, l_mats = [], [], [], [], []
        for h in heads:
            gc = jnp.broadcast_to(gates[:, gc_lane + h:gc_lane + h + 1], (n, n))
            bc = jnp.broadcast_to(gates[:, beta_lane + h:beta_lane + h + 1], (n, n))
            grow = gates_t_ref[gc_lane + h:gc_lane + h + 1, :]
            dec = jnp.exp(jnp.where(incl, gc - grow, -1e30))
            gcol.append(gc)
            bcol.append(bc)
            g_last.append(grow[:, last:last + 1])
            decay.append(dec)
            l_mats.append(jnp.where(strict, kk[h // rep] * dec * bc, 0.0))
        t_inv = _unit_triangular_inverses(l_mats, row, col)
        uw, lhs_state, lhs_new = [], [], []
        for i, h in enumerate(heads):
            kf = kq[h // rep].astype(F32)
            qf = qq[h // rep].astype(F32)
            eg = jnp.exp(gcol[i])
            vf = v_ref[:, h * hd:(h + 1) * hd].astype(F32)
            rhs = jnp.concatenate([(vf * bcol[i]).astype(BF16), (kf * bcol[i] * eg).astype(BF16)], axis=1)
            uw.append(_dot(t_inv[i].astype(BF16), rhs))
            attn = (qk[h // rep] * decay[i]).astype(BF16)
            kd_t = (kf * jnp.exp(g_last[i] - gcol[i])).T.astype(BF16)
            lhs_state.append((qf * eg).astype(BF16))
            lhs_new.append(jnp.concatenate([attn, kd_t], axis=0))
        states = [state_ref[h] for h in heads]
        pq = [_dot(jnp.concatenate([x[:, hd:2 * hd].astype(BF16), qd], axis=0), s.astype(BF16))
              for x, qd, s in zip(uw, lhs_state, states)]
        v_new = [(x[:, 0:hd] - y[0:n]).astype(BF16) for x, y in zip(uw, pq)]
        r = [_dot(a, b) for a, b in zip(lhs_new, v_new)]
        for i, h in enumerate(heads):
            o_ref[:, h * hd:(h + 1) * hd] = pq[i][n:2 * n] + r[i][0:n]
            state_ref[h] = states[i] * jnp.exp(g_last[i]) + r[i][n:2 * n]


def _gdn_scan(q, k, v, gates, gates_t, batch, seq, reverse):
    t = q.shape[0]
    hd = HEAD_DIM
    n = GDN_CHUNK
    nc = seq // n
    qk_w, v_w = GDN_QK_HEADS * hd, GDN_V_HEADS * hd
    blk = (lambda b, c: b * nc + nc - 1 - c) if reverse else (lambda b, c: b * nc + c)
    row = lambda b, c: (blk(b, c), 0)
    return pl.pallas_call(
        functools.partial(_gdn_kernel, reverse=reverse),
        grid=(batch, nc),
        in_specs=[
            pl.BlockSpec((n, qk_w), row),
            pl.BlockSpec((n, qk_w), row),
            pl.BlockSpec((n, v_w), row),
            pl.BlockSpec((n, hd), row),
            pl.BlockSpec((hd, n), lambda b, c: (0, blk(b, c))),
        ],
        out_specs=pl.BlockSpec((n, v_w), row),
        out_shape=jax.ShapeDtypeStruct((t, v_w), F32),
        scratch_shapes=[pltpu.VMEM((GDN_V_HEADS, hd, hd), F32)],
        compiler_params=_params("parallel", "arbitrary"),
        name="gdn_scan_bwd" if reverse else "gdn_scan_fwd",
    )(q, k, v, gates, gates_t)


def _gdn_outproj_kernel(of_ref, ob_ref, z_ref, g_ref, w_ref, res_ref, o_ref, act_ref):
    hd = HEAD_DIM
    for h in range(GDN_V_HEADS):
        cols = slice(h * hd, (h + 1) * hd)
        o = _rms(of_ref[:, cols] + ob_ref[:, cols], g_ref[...])
        z = z_ref[:, cols].astype(F32)
        act_ref[:, cols] = (o * (z * _sigmoid(z))).astype(BF16)
    o_ref[...] = res_ref[...] + _dot(act_ref[...], w_ref[...])


def _gdn_outproj(o_f, o_b, z, gain, w, res):
    t, k = o_f.shape
    d = w.shape[1]
    tm = TM_PROJ
    row = lambda i: (i, 0)
    return pl.pallas_call(
        _gdn_outproj_kernel,
        grid=(t // tm,),
        in_specs=[
            pl.BlockSpec((tm, k), row),
            pl.BlockSpec((tm, k), row),
            pl.BlockSpec((tm, k), row),
            _const_spec((1, HEAD_DIM)),
            _const_spec((k, d)),
            pl.BlockSpec((tm, d), row),
        ],
        out_specs=pl.BlockSpec((tm, d), row),
        out_shape=jax.ShapeDtypeStruct((t, d), F32),
        scratch_shapes=[pltpu.VMEM((tm, k), BF16)],
        compiler_params=_params("parallel"),
        name="gdn_outproj",
    )(o_f, o_b, z, gain, w, res)


def _rope_tables(seq):
    t = jnp.arange(seq)
    row = (t // ROPE_GRID_W).astype(F32)
    col = (t % ROPE_GRID_W).astype(F32)
    half = HEAD_DIM // 2
    n_freq = half // 2
    inv_freq = ROPE_THETA ** (-(jnp.arange(n_freq, dtype=F32) * 2.0 / half))
    ang_r = row[:, None] * inv_freq[None, :]
    ang_c = col[:, None] * inv_freq[None, :]
    zero = jnp.zeros_like(ang_r)
    cos = jnp.concatenate([jnp.cos(ang_r), jnp.cos(ang_r), jnp.cos(ang_c), jnp.cos(ang_c)], axis=-1)
    sa = jnp.concatenate([-jnp.sin(ang_r), zero, -jnp.sin(ang_c), zero], axis=-1)
    sb = jnp.concatenate([zero, jnp.sin(ang_r), zero, jnp.sin(ang_c)], axis=-1)
    return cos, sa, sb


def kernel(x, norm_mix, norm_ffn, norm_final, attn_w_in, attn_q_norm, attn_k_norm, attn_w_out, gdn_w_in, gdn_conv_w, gdn_conv_b, gdn_a_log, gdn_dt_bias, gdn_o_norm, gdn_w_out, ffn_w_up, ffn_conv_w, ffn_conv_b, ffn_w_down):
    batch, seq, d = x.shape
    t = batch * seq
    hd = HEAD_DIM
    assert seq % TM_PROJ == 0 and seq % (ATT_TK * ATT_KV_UNROLL) == 0 and seq % GDN_CHUNK == 0
    assert seq % ROPE_GRID_W == 0
    h = x.reshape(t, d)
    row = lambda a: a.reshape(1, -1)
    final_gain = row(norm_final)

    rope = _rope_tables(seq)
    q_gain = row(attn_q_norm[0]) * (hd ** -0.5 * LOG2E)
    q, k, v, gate = _attn_inproj(h, row(norm_mix[0]), attn_w_in[0].astype(BF16), q_gain,
                                 row(attn_k_norm[0]), rope, seq)
    vt = v.reshape(batch, seq, ATT_KV_HEADS, hd).transpose(0, 2, 3, 1).reshape(batch * ATT_KV_HEADS, hd, seq)
    vt = jnp.concatenate([vt, jnp.ones((batch * ATT_KV_HEADS, ATT_ONES_ROWS, seq), BF16)], axis=1)
    attn = _flash_attention(q, k, vt, gate, batch, seq)
    h = _outproj(attn, attn_w_out[0].astype(BF16), h)
    h = _ffn(h, row(norm_ffn[0]), ffn_w_up[0].astype(BF16), ffn_conv_w[0], row(ffn_conv_b[0]),
             ffn_w_down[0].astype(BF16), final_gain, seq, False)

    conv_width = GDN_QK_HEADS * hd * 2 + GDN_V_HEADS * hd
    z_width = GDN_V_HEADS * hd
    w_in = gdn_w_in[0]
    w_qkv = w_in[:, :conv_width].astype(BF16)
    w_z = w_in[:, conv_width:conv_width + z_width].astype(BF16)
    w_ab = jnp.pad(w_in[:, conv_width + z_width:], ((0, 0), (0, hd - 4 * GDN_V_HEADS))).astype(BF16)
    pad_gate = lambda p: jnp.pad(jnp.stack([p[0], jnp.zeros_like(p[0]), p[1], jnp.zeros_like(p[1])]).reshape(1, -1),
                                 ((0, 0), (0, hd - 4 * GDN_V_HEADS)))
    gq, gk, gv, gz, gates = _gdn_inproj(h, row(norm_mix[1]), w_qkv, w_z, w_ab, gdn_conv_w[0], row(gdn_conv_b[0]),
                                        pad_gate(gdn_a_log[0]), pad_gate(gdn_dt_bias[0]), seq)
    gates_t = gates.T
    o_f = _gdn_scan(gq, gk, gv, gates, gates_t, batch, seq, False)
    o_b = _gdn_scan(gq, gk, gv, gates, gates_t, batch, seq, True)
    h = _gdn_outproj(o_f, o_b, gz, row(gdn_o_norm[0]), gdn_w_out[0].astype(BF16), h)
    h = _ffn(h, row(norm_ffn[1]), ffn_w_up[1].astype(BF16), ffn_conv_w[1], row(ffn_conv_b[1]),
             ffn_w_down[1].astype(BF16), final_gain, seq, True)
    return h.reshape(batch, seq, d)
```

```python
import functools

import jax
import jax.numpy as jnp
from jax import lax
from jax.experimental import pallas as pl
from jax.experimental.pallas import tpu as pltpu

F32 = jnp.float32
BF16 = jnp.bfloat16

EPS = 1e-6
ROPE_THETA = 10000.0
ROPE_GRID_W = 64

HEAD_DIM = 128
ATT_HEADS = 8
ATT_KV_HEADS = 2
GDN_QK_HEADS = 8
GDN_V_HEADS = 16
GDN_CHUNK = 128
GDN_DIAG = 16
GDN_HEAD_GROUP = 16

HALO = 8
MXU_N = 256
V7X_VMEM_LIMIT = 56 * 1024 * 1024

TM_PROJ = 512
TM_GDN_IN = 256
ATT_TQ = 256
ATT_TK = 512
ATT_KV_UNROLL = 8
ATT_LOOKAHEAD = 4
ATT_ONES_ROWS = 16
LOG2E = 1.4426950408889634


def _dot(a, b):
    return jnp.dot(a, b, preferred_element_type=F32)


def _dot_nt(a, b):
    return lax.dot_general(a, b, (((1,), (1,)), ((), ())), preferred_element_type=F32)


def _rms(x, gain):
    return x * lax.rsqrt(jnp.mean(x * x, axis=-1, keepdims=True) + EPS) * gain


def _sigmoid(x):
    return 1.0 / (1.0 + jnp.exp2(x * (-LOG2E)))


def _const_spec(shape):
    return pl.BlockSpec(shape, lambda *_: (0,) * len(shape), pipeline_mode=pl.Buffered(1))


def _params(*semantics):
    return pltpu.CompilerParams(dimension_semantics=semantics, vmem_limit_bytes=V7X_VMEM_LIMIT)


def _halo_specs(tm, d, n_tiles):
    r = tm // HALO
    last = n_tiles * r - 1
    return [
        pl.BlockSpec((tm, d), lambda i: (i, 0)),
        pl.BlockSpec((HALO, d), lambda i: (jnp.maximum(i * r - 1, 0), 0)),
        pl.BlockSpec((HALO, d), lambda i: (jnp.minimum((i + 1) * r, last), 0)),
    ]


def _normed_with_halo(x_ref, xp_ref, xn_ref, gain, seq_tiles):
    i = pl.program_id(0)
    pos = i % seq_tiles
    y = _rms(x_ref[...], gain)
    yn = jnp.where(pos == seq_tiles - 1, 0.0, _rms(xn_ref[...], gain))
    yp = jnp.where(pos == 0, 0.0, _rms(xp_ref[...], gain))
    return jnp.concatenate([y, yn, yp], axis=0).astype(BF16)


def _shift_rows(u, k):
    rows, cols = u.shape
    nb = rows // HALO
    blocks = u.reshape(nb, HALO, cols)
    sub = lax.broadcasted_iota(jnp.int32, (1, HALO, cols), 1)
    if k > 0:
        mixed = jnp.where(sub >= HALO - k, jnp.concatenate([blocks[nb - 1:], blocks[:nb - 1]], axis=0), blocks)
    else:
        mixed = jnp.where(sub < -k, jnp.concatenate([blocks[1:], blocks[:1]], axis=0), blocks)
    return pltpu.roll(mixed, k % HALO, axis=1).reshape(rows, cols)


def _seq_conv(u, w_ref, b_ref, cols, tm):
    taps = w_ref.shape[0]
    half = taps // 2
    acc = u * w_ref[half:half + 1, cols]
    for t in range(taps):
        if t != half:
            acc = acc + _shift_rows(u, half - t) * w_ref[t:t + 1, cols]
    return acc[0:tm] + b_ref[:, cols]


def _attn_inproj_kernel(x_ref, g_ref, w_ref, qg_ref, kg_ref, cos_ref, sa_ref, sb_ref,
                        q_ref, k_ref, v_ref, gate_ref):
    hd = HEAD_DIM
    yb = _rms(x_ref[...], g_ref[...]).astype(BF16)
    cos, sa, sb = cos_ref[...], sa_ref[...], sb_ref[...]
    q_w = ATT_HEADS * hd
    kv_w = ATT_KV_HEADS * hd

    def norm_rope(z, gain):
        zn = _rms(z, gain)
        return zn * cos + pltpu.roll(zn, hd - hd // 4, 1) * sa + pltpu.roll(zn, hd // 4, 1) * sb

    for c in range(q_w // MXU_N):
        z = _dot(yb, w_ref[:, c * MXU_N:(c + 1) * MXU_N])
        for e in range(MXU_N // hd):
            col = c * MXU_N + e * hd
            q_ref[:, col:col + hd] = norm_rope(z[:, e * hd:(e + 1) * hd], qg_ref[...]).astype(BF16)
    for c in range(kv_w // MXU_N):
        z = _dot(yb, w_ref[:, q_w + c * MXU_N:q_w + (c + 1) * MXU_N])
        for e in range(MXU_N // hd):
            col = c * MXU_N + e * hd
            k_ref[:, col:col + hd] = norm_rope(z[:, e * hd:(e + 1) * hd], kg_ref[...]).astype(BF16)
    for c in range(kv_w // MXU_N):
        lo = q_w + kv_w + c * MXU_N
        v_ref[:, c * MXU_N:(c + 1) * MXU_N] = _dot(yb, w_ref[:, lo:lo + MXU_N]).astype(BF16)
    for c in range(q_w // MXU_N):
        lo = q_w + 2 * kv_w + c * MXU_N
        gate_ref[:, c * MXU_N:(c + 1) * MXU_N] = _sigmoid(_dot(yb, w_ref[:, lo:lo + MXU_N])).astype(BF16)


def _attn_inproj(h, gain, w_in, q_gain, k_gain, rope, seq):
    t, d = h.shape
    tm = TM_PROJ
    n_in = w_in.shape[1]
    q_w, kv_w = ATT_HEADS * HEAD_DIM, ATT_KV_HEADS * HEAD_DIM
    seq_tiles = seq // tm
    row = lambda i: (i, 0)
    pos = lambda i: (i % seq_tiles, 0)
    return pl.pallas_call(
        _attn_inproj_kernel,
        grid=(t // tm,),
        in_specs=[
            pl.BlockSpec((tm, d), row),
            _const_spec((1, d)),
            _const_spec((d, n_in)),
            _const_spec((1, HEAD_DIM)),
            _const_spec((1, HEAD_DIM)),
            pl.BlockSpec((tm, HEAD_DIM), pos),
            pl.BlockSpec((tm, HEAD_DIM), pos),
            pl.BlockSpec((tm, HEAD_DIM), pos),
        ],
        out_specs=[
            pl.BlockSpec((tm, q_w), row),
            pl.BlockSpec((tm, kv_w), row),
            pl.BlockSpec((tm, kv_w), row),
            pl.BlockSpec((tm, q_w), row),
        ],
        out_shape=[
            jax.ShapeDtypeStruct((t, q_w), BF16),
            jax.ShapeDtypeStruct((t, kv_w), BF16),
            jax.ShapeDtypeStruct((t, kv_w), BF16),
            jax.ShapeDtypeStruct((t, q_w), BF16),
        ],
        compiler_params=_params("parallel"),
        name="attn_inproj",
    )(h, gain, w_in, q_gain, k_gain, *rope)


def _flash_kernel(q_ref, k_ref, vt_ref, gate_ref, o_ref, qt_ref, m_ref, acc_ref, *, seq):
    hd = HEAD_DIM
    tk = ATT_TK
    group = q_ref.shape[1] // hd
    for g in range(group):
        qt_ref[g] = q_ref[:, g * hd:(g + 1) * hd].astype(F32).T.astype(BF16)
    m_ref[...] = jnp.full(m_ref.shape, -jnp.inf, F32)
    acc_ref[...] = jnp.zeros(acc_ref.shape, F32)

    def scores(unit):
        start, g = unit
        return _dot(k_ref[pl.ds(start, tk), :], qt_ref[g])

    def step(j, carry):
        units = []
        for u in range(ATT_KV_UNROLL):
            start = pl.multiple_of((j * ATT_KV_UNROLL + u) * tk, tk)
            units += [(start, g) for g in range(group)]
        pending = [scores(unit) for unit in units[:ATT_LOOKAHEAD]]
        for idx, (start, g) in enumerate(units):
            s = pending.pop(0)
            if idx + ATT_LOOKAHEAD < len(units):
                pending.append(scores(units[idx + ATT_LOOKAHEAD]))
            m_prev = m_ref[g]
            m_new = jnp.maximum(m_prev, jnp.max(s, axis=0, keepdims=True))
            p = jnp.exp2(s - m_new).astype(BF16)
            acc_ref[g] = acc_ref[g] * jnp.exp2(m_prev - m_new) + _dot(vt_ref[:, pl.ds(start, tk)], p)
            m_ref[g] = m_new
        return carry

    lax.fori_loop(0, seq // (tk * ATT_KV_UNROLL), step, 0)
    for g in range(group):
        acc = acc_ref[g]
        o = (acc[0:hd] / acc[hd:hd + 1]).T
        o_ref[:, g * hd:(g + 1) * hd] = (o * gate_ref[:, g * hd:(g + 1) * hd].astype(F32)).astype(BF16)


def _flash_attention(q, k, vt, gate, batch, seq):
    t = q.shape[0]
    hd = HEAD_DIM
    group = ATT_HEADS // ATT_KV_HEADS
    tq = ATT_TQ
    nq = seq // tq
    rows = vt.shape[1]
    qmap = lambda b, g, i: (b * nq + i, g)
    return pl.pallas_call(
        functools.partial(_flash_kernel, seq=seq),
        grid=(batch, ATT_KV_HEADS, nq),
        in_specs=[
            pl.BlockSpec((tq, group * hd), qmap),
            pl.BlockSpec((seq, hd), lambda b, g, i: (b, g)),
            pl.BlockSpec((None, rows, seq), lambda b, g, i: (b * ATT_KV_HEADS + g, 0, 0)),
            pl.BlockSpec((tq, group * hd), qmap),
        ],
        out_specs=pl.BlockSpec((tq, group * hd), qmap),
        out_shape=jax.ShapeDtypeStruct((t, ATT_HEADS * hd), BF16),
        scratch_shapes=[
            pltpu.VMEM((group, hd, tq), BF16),
            pltpu.VMEM((group, 1, tq), F32),
            pltpu.VMEM((group, rows, tq), F32),
        ],
        compiler_params=_params("parallel", "parallel", "parallel"),
        name="flash_attention",
    )(q, k, vt, gate)


def _outproj_kernel(a_ref, w_ref, res_ref, o_ref):
    o_ref[...] = res_ref[...] + _dot(a_ref[...], w_ref[...])


def _outproj(a, w, res):
    t, k = a.shape
    d = w.shape[1]
    tm = TM_PROJ
    row = lambda i: (i, 0)
    return pl.pallas_call(
        _outproj_kernel,
        grid=(t // tm,),
        in_specs=[pl.BlockSpec((tm, k), row), _const_spec((k, d)), pl.BlockSpec((tm, d), row)],
        out_specs=pl.BlockSpec((tm, d), row),
        out_shape=jax.ShapeDtypeStruct((t, d), F32),
        compiler_params=_params("parallel"),
        name="attn_outproj",
    )(a, w, res)


def _ffn_kernel(x_ref, xp_ref, xn_ref, g_ref, wup_ref, cw_ref, cb_ref, wdn_ref, gf_ref, o_ref, act_ref,
                *, seq_tiles, final_norm):
    tm = x_ref.shape[0]
    d_ff = wdn_ref.shape[0]
    y_ext = _normed_with_halo(x_ref, xp_ref, xn_ref, g_ref[...], seq_tiles)
    for c in range(d_ff // MXU_N):
        ca = slice(c * MXU_N, (c + 1) * MXU_N)
        cb = slice(d_ff + c * MXU_N, d_ff + (c + 1) * MXU_N)
        a = _seq_conv(_dot(y_ext, wup_ref[:, ca]), cw_ref, cb_ref, ca, tm)
        b = _seq_conv(_dot(y_ext, wup_ref[:, cb]), cw_ref, cb_ref, cb, tm)
        act_ref[:, ca] = (a * _sigmoid(a) * b).astype(BF16)
    out = x_ref[...] + _dot(act_ref[...], wdn_ref[...])
    if final_norm:
        out = _rms(out, gf_ref[...])
    o_ref[...] = out


def _ffn(h, gain, w_up, conv_w, conv_b, w_down, final_gain, seq, final_norm):
    t, d = h.shape
    tm = TM_PROJ
    d_ff = w_down.shape[0]
    n_tiles = t // tm
    return pl.pallas_call(
        functools.partial(_ffn_kernel, seq_tiles=seq // tm, final_norm=final_norm),
        grid=(n_tiles,),
        in_specs=_halo_specs(tm, d, n_tiles) + [
            _const_spec((1, d)),
            _const_spec((d, 2 * d_ff)),
            _const_spec(conv_w.shape),
            _const_spec((1, 2 * d_ff)),
            _const_spec((d_ff, d)),
            _const_spec((1, d)),
        ],
        out_specs=pl.BlockSpec((tm, d), lambda i: (i, 0)),
        out_shape=jax.ShapeDtypeStruct((t, d), F32),
        scratch_shapes=[pltpu.VMEM((tm, d_ff), BF16)],
        compiler_params=_params("parallel"),
        name="conv_ffn",
    )(h, h, h, gain, w_up, conv_w, conv_b, w_down, final_gain)


def _gdn_inproj_kernel(x_ref, xp_ref, xn_ref, g_ref, w_ref, cw_ref, cb_ref, alog_ref, dtb_ref, lower_ref, upper_ref,
                       q_ref, k_ref, v_ref, z_ref, gates_ref, *, seq_tiles):
    hd = HEAD_DIM
    tm = x_ref.shape[0]
    nh = GDN_V_HEADS
    qk_w, v_w = GDN_QK_HEADS * hd, GDN_V_HEADS * hd
    conv_w = 2 * qk_w + v_w
    y_ext = _normed_with_halo(x_ref, xp_ref, xn_ref, g_ref[...], seq_tiles)
    y = y_ext[0:tm]

    def l2n(z, scale=1.0):
        return z * (lax.rsqrt(jnp.sum(z * z, axis=-1, keepdims=True) + EPS) * scale)

    for c in range(conv_w // MXU_N):
        cols = slice(c * MXU_N, (c + 1) * MXU_N)
        u = _seq_conv(_dot(y_ext, w_ref[:, cols]), cw_ref, cb_ref, cols, tm)
        u = u * _sigmoid(u)
        lo = c * MXU_N
        if lo < qk_w:
            for e in range(MXU_N // hd):
                q_ref[:, lo + e * hd:lo + (e + 1) * hd] = l2n(u[:, e * hd:(e + 1) * hd], hd ** -0.5).astype(BF16)
        elif lo < 2 * qk_w:
            for e in range(MXU_N // hd):
                k_ref[:, lo - qk_w + e * hd:lo - qk_w + (e + 1) * hd] = l2n(u[:, e * hd:(e + 1) * hd]).astype(BF16)
        else:
            v_ref[:, lo - 2 * qk_w:lo - 2 * qk_w + MXU_N] = u.astype(BF16)
    for c in range(v_w // MXU_N):
        cols = slice(c * MXU_N, (c + 1) * MXU_N)
        z_ref[:, cols] = _dot(y, w_ref[:, conv_w + c * MXU_N:conv_w + (c + 1) * MXU_N]).astype(BF16)

    ab = _dot(y, w_ref[:, conv_w + v_w:conv_w + v_w + 4 * nh])
    ab = jnp.concatenate([ab, jnp.zeros((tm, hd - 4 * nh), F32)], axis=1)
    lane = lax.broadcasted_iota(jnp.int32, ab.shape, 1)
    xa = ab + dtb_ref[...]
    softplus = jnp.maximum(xa, 0.0) + jnp.log(1.0 + jnp.exp(-jnp.abs(xa)))
    g = -jnp.exp(alog_ref[...]) * softplus
    beta = _sigmoid(ab)
    g1 = g.astype(BF16)
    r1 = g - g1.astype(F32)
    g2 = r1.astype(BF16)
    g3 = (r1 - g2.astype(F32)).astype(BF16)
    gs = jnp.concatenate([g1, g2, g3], axis=1)
    cf = _dot(lower_ref[...], gs)
    cb = _dot(upper_ref[...], gs)
    cum_f = cf[:, 0:hd] + cf[:, hd:2 * hd] + cf[:, 2 * hd:3 * hd]
    cum_b = cb[:, 0:hd] + cb[:, hd:2 * hd] + cb[:, 2 * hd:3 * hd]
    gates_ref[...] = jnp.where(lane < nh, cum_f, jnp.where((lane >= 2 * nh) & (lane < 3 * nh), cum_b, beta))


def _gdn_inproj(h, gain, w_in, conv_w, conv_b, alog_row, dtb_row, seq):
    t, d = h.shape
    tm = TM_GDN_IN
    hd = HEAD_DIM
    qk_w, v_w = GDN_QK_HEADS * hd, GDN_V_HEADS * hd
    n_tiles = t // tm
    row = lambda i: (i, 0)
    r = lax.broadcasted_iota(jnp.int32, (tm, tm), 0)
    c = lax.broadcasted_iota(jnp.int32, (tm, tm), 1)
    same = (r // GDN_CHUNK) == (c // GDN_CHUNK)
    lower = (same & (c <= r)).astype(BF16)
    upper = (same & (c >= r)).astype(BF16)
    return pl.pallas_call(
        functools.partial(_gdn_inproj_kernel, seq_tiles=seq // tm),
        grid=(n_tiles,),
        in_specs=_halo_specs(tm, d, n_tiles) + [
            _const_spec((1, d)),
            _const_spec(w_in.shape),
            _const_spec(conv_w.shape),
            _const_spec(conv_b.shape),
            _const_spec((1, hd)),
            _const_spec((1, hd)),
            _const_spec((tm, tm)),
            _const_spec((tm, tm)),
        ],
        out_specs=[
            pl.BlockSpec((tm, qk_w), row),
            pl.BlockSpec((tm, qk_w), row),
            pl.BlockSpec((tm, v_w), row),
            pl.BlockSpec((tm, v_w), row),
            pl.BlockSpec((tm, hd), row),
        ],
        out_shape=[
            jax.ShapeDtypeStruct((t, qk_w), BF16),
            jax.ShapeDtypeStruct((t, qk_w), BF16),
            jax.ShapeDtypeStruct((t, v_w), BF16),
            jax.ShapeDtypeStruct((t, v_w), BF16),
            jax.ShapeDtypeStruct((t, hd), F32),
        ],
        compiler_params=_params("parallel"),
        name="gdn_inproj",
    )(h, h, h, gain, w_in, conv_w, conv_b, alog_row, dtb_row, lower, upper)


def _unit_triangular_inverses(l_mats, row, col):
    n = l_mats[0].shape[0]
    size = GDN_DIAG
    same = (row // size) == (col // size)
    eye = jnp.where(row == col, 1.0, 0.0)
    lds = [jnp.where(same, l, 0.0) for l in l_mats]
    ldbs = [ld.astype(BF16) for ld in lds]
    qs = [_dot(ldb, ldb) for ldb in ldbs]
    ps = [eye - ld for ld in lds]
    power = 2
    while 2 * power < size:
        qbs = [q.astype(BF16) for q in qs]
        xs = [_dot(qb, jnp.concatenate([qb, p.astype(BF16)], axis=1)) for qb, p in zip(qbs, ps)]
        qs = [x[:, 0:n] for x in xs]
        ps = [p + x[:, n:2 * n] for p, x in zip(ps, xs)]
        power *= 2
    ds = [p + _dot(q.astype(BF16), p.astype(BF16)) for q, p in zip(qs, ps)]
    while size < n:
        pair = (row // (2 * size)) == (col // (2 * size))
        join = pair & jnp.logical_not(same)
        dbs = [d.astype(BF16) for d in ds]
        es = [_dot(jnp.where(join, l, 0.0).astype(BF16), db).astype(BF16) for l, db in zip(l_mats, dbs)]
        ds = [d - _dot(db, e) for d, db, e in zip(ds, dbs, es)]
        same = pair
        size *= 2
    return ds


def _gdn_kernel(q_ref, k_ref, v_ref, gates_ref, gates_t_ref, o_ref, state_ref, *, reverse):
    hd = HEAD_DIM
    n = GDN_CHUNK
    nh = GDN_V_HEADS
    rep = GDN_V_HEADS // GDN_QK_HEADS

    @pl.when(pl.program_id(1) == 0)
    def _():
        state_ref[...] = jnp.zeros(state_ref.shape, F32)

    row = lax.broadcasted_iota(jnp.int32, (n, n), 0)
    col = lax.broadcasted_iota(jnp.int32, (n, n), 1)
    incl = (col >= row) if reverse else (col <= row)
    strict = (col > row) if reverse else (col < row)
    last = 0 if reverse else n - 1
    gc_lane = 2 * nh if reverse else 0
    beta_lane = 3 * nh if reverse else nh
    gates = gates_ref[...]

    for first in range(0, nh, GDN_HEAD_GROUP):
        heads = list(range(first, first + GDN_HEAD_GROUP))
        qk_heads = sorted({h // rep for h in heads})
        kq = {hq: k_ref[:, hq * hd:(hq + 1) * hd] for hq in qk_heads}
        qq = {hq: q_ref[:, hq * hd:(hq + 1) * hd] for hq in qk_heads}
        kk = {hq: _dot_nt(kq[hq], kq[hq]) for hq in qk_heads}
        qk = {hq: _dot_nt(qq[hq], kq[hq]) for hq in qk_heads}
        gcol, bcol, g_last, decay, l_mats = [], [], [], [], []
        for h in heads:
            gc = jnp.broadcast_to(gates[:, gc_lane + h:gc_lane + h + 1], (n, n))
            bc = jnp.broadcast_to(gates[:, beta_lane + h:beta_lane + h + 1], (n, n))
            grow = gates_t_ref[gc_lane + h:gc_lane + h + 1, :]
            dec = jnp.exp(jnp.where(incl, gc - grow, -1e30))
            gcol.append(gc)
            bcol.append(bc)
            g_last.append(grow[:, last:last + 1])
            decay.append(dec)
            l_mats.append(jnp.where(strict, kk[h // rep] * dec * bc, 0.0))
        t_inv = _unit_triangular_inverses(l_mats, row, col)
        uw, lhs_state, lhs_new = [], [], []
        for i, h in enumerate(heads):
            kf = kq[h // rep].astype(F32)
            qf = qq[h // rep].astype(F32)
            eg = jnp.exp(gcol[i])
            vf = v_ref[:, h * hd:(h + 1) * hd].astype(F32)
            rhs = jnp.concatenate([(vf * bcol[i]).astype(BF16), (kf * bcol[i] * eg).astype(BF16)], axis=1)
            uw.append(_dot(t_inv[i].astype(BF16), rhs))
            attn = (qk[h // rep] * decay[i]).astype(BF16)
            kd_t = (kf * jnp.exp(g_last[i] - gcol[i])).T.astype(BF16)
            lhs_state.append((qf * eg).astype(BF16))
            lhs_new.append(jnp.concatenate([attn, kd_t], axis=0))
        states = [state_ref[h] for h in heads]
        pq = [_dot(jnp.concatenate([x[:, hd:2 * hd].astype(BF16), qd], axis=0), s.astype(BF16))
              for x, qd, s in zip(uw, lhs_state, states)]
        v_new = [(x[:, 0:hd] - y[0:n]).astype(BF16) for x, y in zip(uw, pq)]
        r = [_dot(a, b) for a, b in zip(lhs_new, v_new)]
        for i, h in enumerate(heads):
            o_ref[:, h * hd:(h + 1) * hd] = pq[i][n:2 * n] + r[i][0:n]
            state_ref[h] = states[i] * jnp.exp(g_last[i]) + r[i][n:2 * n]


def _gdn_scan(q, k, v, gates, gates_t, batch, seq, reverse):
    t = q.shape[0]
    hd = HEAD_DIM
    n = GDN_CHUNK
    nc = seq // n
    qk_w, v_w = GDN_QK_HEADS * hd, GDN_V_HEADS * hd
    blk = (lambda b, c: b * nc + nc - 1 - c) if reverse else (lambda b, c: b * nc + c)
    row = lambda b, c: (blk(b, c), 0)
    return pl.pallas_call(
        functools.partial(_gdn_kernel, reverse=reverse),
        grid=(batch, nc),
        in_specs=[
            pl.BlockSpec((n, qk_w), row),
            pl.BlockSpec((n, qk_w), row),
            pl.BlockSpec((n, v_w), row),
            pl.BlockSpec((n, hd), row),
            pl.BlockSpec((hd, n), lambda b, c: (0, blk(b, c))),
        ],
        out_specs=pl.BlockSpec((n, v_w), row),
        out_shape=jax.ShapeDtypeStruct((t, v_w), F32),
        scratch_shapes=[pltpu.VMEM((GDN_V_HEADS, hd, hd), F32)],
        compiler_params=_params("parallel", "arbitrary"),
        name="gdn_scan_bwd" if reverse else "gdn_scan_fwd",
    )(q, k, v, gates, gates_t)


def _gdn_outproj_kernel(of_ref, ob_ref, z_ref, g_ref, w_ref, res_ref, o_ref, act_ref):
    hd = HEAD_DIM
    for h in range(GDN_V_HEADS):
        cols = slice(h * hd, (h + 1) * hd)
        o = _rms(of_ref[:, cols] + ob_ref[:, cols], g_ref[...])
        z = z_ref[:, cols].astype(F32)
        act_ref[:, cols] = (o * (z * _sigmoid(z))).astype(BF16)
    o_ref[...] = res_ref[...] + _dot(act_ref[...], w_ref[...])


def _gdn_outproj(o_f, o_b, z, gain, w, res):
    t, k = o_f.shape
    d = w.shape[1]
    tm = TM_PROJ
    row = lambda i: (i, 0)
    return pl.pallas_call(
        _gdn_outproj_kernel,
        grid=(t // tm,),
        in_specs=[
            pl.BlockSpec((tm, k), row),
            pl.BlockSpec((tm, k), row),
            pl.BlockSpec((tm, k), row),
            _const_spec((1, HEAD_DIM)),
            _const_spec((k, d)),
            pl.BlockSpec((tm, d), row),
        ],
        out_specs=pl.BlockSpec((tm, d), row),
        out_shape=jax.ShapeDtypeStruct((t, d), F32),
        scratch_shapes=[pltpu.VMEM((tm, k), BF16)],
        compiler_params=_params("parallel"),
        name="gdn_outproj",
    )(o_f, o_b, z, gain, w, res)


def _rope_tables(seq):
    grid_rows = seq // ROPE_GRID_W
    half = HEAD_DIM // 2
    n_freq = half // 2
    inv_freq = ROPE_THETA ** (-(jnp.arange(n_freq, dtype=F32) * 2.0 / half))
    ang_r = jnp.arange(grid_rows, dtype=F32)[:, None] * inv_freq[None, :]
    ang_c = jnp.arange(ROPE_GRID_W, dtype=F32)[:, None] * inv_freq[None, :]
    cos_r, sin_r, cos_c, sin_c = lax.optimization_barrier(
        (jnp.cos(ang_r), jnp.sin(ang_r), jnp.cos(ang_c), jnp.sin(ang_c)))
    by_row = lambda a: jnp.repeat(a, ROPE_GRID_W, axis=0)
    by_col = lambda a: jnp.tile(a, (grid_rows, 1))
    zero = jnp.zeros((seq, n_freq), F32)
    cos = jnp.concatenate([by_row(cos_r), by_row(cos_r), by_col(cos_c), by_col(cos_c)], axis=-1)
    sa = jnp.concatenate([-by_row(sin_r), zero, -by_col(sin_c), zero], axis=-1)
    sb = jnp.concatenate([zero, by_row(sin_r), zero, by_col(sin_c)], axis=-1)
    return cos, sa, sb


def kernel(x, norm_mix, norm_ffn, norm_final, attn_w_in, attn_q_norm, attn_k_norm, attn_w_out, gdn_w_in, gdn_conv_w, gdn_conv_b, gdn_a_log, gdn_dt_bias, gdn_o_norm, gdn_w_out, ffn_w_up, ffn_conv_w, ffn_conv_b, ffn_w_down):
    batch, seq, d = x.shape
    t = batch * seq
    hd = HEAD_DIM
    assert seq % TM_PROJ == 0 and seq % (ATT_TK * ATT_KV_UNROLL) == 0 and seq % GDN_CHUNK == 0
    assert seq % ROPE_GRID_W == 0
    h = x.reshape(t, d)
    row = lambda a: a.reshape(1, -1)
    final_gain = row(norm_final)

    rope = _rope_tables(seq)
    q_gain = row(attn_q_norm[0]) * (hd ** -0.5 * LOG2E)
    q, k, v, gate = _attn_inproj(h, row(norm_mix[0]), attn_w_in[0].astype(BF16), q_gain,
                                 row(attn_k_norm[0]), rope, seq)
    vt = v.reshape(batch, seq, ATT_KV_HEADS, hd).transpose(0, 2, 3, 1).reshape(batch * ATT_KV_HEADS, hd, seq)
    vt = jnp.concatenate([vt, jnp.ones((batch * ATT_KV_HEADS, ATT_ONES_ROWS, seq), BF16)], axis=1)
    attn = _flash_attention(q, k, vt, gate, batch, seq)
    h = _outproj(attn, attn_w_out[0].astype(BF16), h)
    h = _ffn(h, row(norm_ffn[0]), ffn_w_up[0].astype(BF16), ffn_conv_w[0], row(ffn_conv_b[0]),
             ffn_w_down[0].astype(BF16), final_gain, seq, False)

    pad_gate = lambda p: jnp.pad(jnp.stack([p[0], jnp.zeros_like(p[0]), p[1], jnp.zeros_like(p[1])]).reshape(1, -1),
                                 ((0, 0), (0, hd - 4 * GDN_V_HEADS)))
    gq, gk, gv, gz, gates = _gdn_inproj(h, row(norm_mix[1]), gdn_w_in[0].astype(BF16), gdn_conv_w[0],
                                        row(gdn_conv_b[0]), pad_gate(gdn_a_log[0]), pad_gate(gdn_dt_bias[0]), seq)
    gates_t = gates.T
    o_f = _gdn_scan(gq, gk, gv, gates, gates_t, batch, seq, False)
    o_b = _gdn_scan(gq, gk, gv, gates, gates_t, batch, seq, True)
    h = _gdn_outproj(o_f, o_b, gz, row(gdn_o_norm[0]), gdn_w_out[0].astype(BF16), h)
    h = _ffn(h, row(norm_ffn[1]), ffn_w_up[1].astype(BF16), ffn_conv_w[1], row(ffn_conv_b[1]),
             ffn_w_down[1].astype(BF16), final_gain, seq, True)
    return h.reshape(batch, seq, d)
```

```python
import functools

import jax
import jax.numpy as jnp
from jax import lax
from jax.experimental import pallas as pl
from jax.experimental.pallas import tpu as pltpu

F32 = jnp.float32
BF16 = jnp.bfloat16

EPS = 1e-6
ROPE_THETA = 10000.0
ROPE_GRID_W = 64

HEAD_DIM = 128
ATT_HEADS = 8
ATT_KV_HEADS = 2
GDN_QK_HEADS = 8
GDN_V_HEADS = 16
GDN_CHUNK = 128
GDN_DIAG = 16
GDN_HEAD_GROUP = 16

HALO = 8
MXU_N = 256
V7X_VMEM_LIMIT = 56 * 1024 * 1024

TM_PROJ = 512
TM_GDN_IN = 256
ATT_TQ = 256
ATT_BOUNDED_UNITS = (1024, 4, 2)
ATT_ONLINE_UNITS = (512, 8, 4)
ATT_MAX_SHIFT = 50.0
ATT_BOUND_SLACK = 1.02
LOG2E = 1.4426950408889634


def _dot(a, b):
    return jnp.dot(a, b, preferred_element_type=F32)


def _dot_nt(a, b):
    return lax.dot_general(a, b, (((1,), (1,)), ((), ())), preferred_element_type=F32)


def _rms(x, gain):
    return x * lax.rsqrt(jnp.mean(x * x, axis=-1, keepdims=True) + EPS) * gain


def _sigmoid(x):
    return 1.0 / (1.0 + jnp.exp2(x * (-LOG2E)))


def _const_spec(shape):
    return pl.BlockSpec(shape, lambda *_: (0,) * len(shape), pipeline_mode=pl.Buffered(1))


def _layer_spec(shape, layer):
    return pl.BlockSpec((None,) + tuple(shape[1:]), lambda *_: (layer,) + (0,) * (len(shape) - 1),
                        pipeline_mode=pl.Buffered(1))


def _params(*semantics):
    return pltpu.CompilerParams(dimension_semantics=semantics, vmem_limit_bytes=V7X_VMEM_LIMIT)


def _halo_specs(tm, d, n_tiles):
    r = tm // HALO
    last = n_tiles * r - 1
    return [
        pl.BlockSpec((tm, d), lambda i: (i, 0)),
        pl.BlockSpec((HALO, d), lambda i: (jnp.maximum(i * r - 1, 0), 0)),
        pl.BlockSpec((HALO, d), lambda i: (jnp.minimum((i + 1) * r, last), 0)),
    ]


def _normed_with_halo(x_ref, xp_ref, xn_ref, gain, seq_tiles):
    i = pl.program_id(0)
    pos = i % seq_tiles
    y = _rms(x_ref[...], gain)
    yn = jnp.where(pos == seq_tiles - 1, 0.0, _rms(xn_ref[...], gain))
    yp = jnp.where(pos == 0, 0.0, _rms(xp_ref[...], gain))
    return jnp.concatenate([y, yn, yp], axis=0).astype(BF16)


def _shift_rows(u, k):
    rows, cols = u.shape
    nb = rows // HALO
    blocks = u.reshape(nb, HALO, cols)
    sub = lax.broadcasted_iota(jnp.int32, (1, HALO, cols), 1)
    if k > 0:
        mixed = jnp.where(sub >= HALO - k, jnp.concatenate([blocks[nb - 1:], blocks[:nb - 1]], axis=0), blocks)
    else:
        mixed = jnp.where(sub < -k, jnp.concatenate([blocks[1:], blocks[:1]], axis=0), blocks)
    return pltpu.roll(mixed, k % HALO, axis=1).reshape(rows, cols)


def _seq_conv(u, w_ref, b_ref, cols, tm):
    taps = w_ref.shape[0]
    half = taps // 2
    acc = u * w_ref[half:half + 1, cols]
    for t in range(taps):
        if t != half:
            acc = acc + _shift_rows(u, half - t) * w_ref[t:t + 1, cols]
    return acc[0:tm] + b_ref[:, cols]


def _attn_inproj_kernel(x_ref, g_ref, w_ref, qg_ref, kg_ref, cos_ref, sa_ref, sb_ref,
                        q_ref, k_ref, v_ref, gate_ref):
    hd = HEAD_DIM
    yb = _rms(x_ref[...], g_ref[...]).astype(BF16)
    cos, sa, sb = cos_ref[...], sa_ref[...], sb_ref[...]
    q_w = ATT_HEADS * hd
    kv_w = ATT_KV_HEADS * hd

    def norm_rope(z, gain):
        zn = _rms(z, gain)
        return zn * cos + pltpu.roll(zn, hd - hd // 4, 1) * sa + pltpu.roll(zn, hd // 4, 1) * sb

    for c in range(q_w // MXU_N):
        z = _dot(yb, w_ref[:, c * MXU_N:(c + 1) * MXU_N])
        for e in range(MXU_N // hd):
            col = c * MXU_N + e * hd
            q_ref[:, col:col + hd] = norm_rope(z[:, e * hd:(e + 1) * hd], qg_ref[...]).astype(BF16)
    for c in range(kv_w // MXU_N):
        z = _dot(yb, w_ref[:, q_w + c * MXU_N:q_w + (c + 1) * MXU_N])
        for e in range(MXU_N // hd):
            col = c * MXU_N + e * hd
            k_ref[:, col:col + hd] = norm_rope(z[:, e * hd:(e + 1) * hd], kg_ref[...]).astype(BF16)
    for c in range(kv_w // MXU_N):
        lo = q_w + kv_w + c * MXU_N
        v_ref[:, c * MXU_N:(c + 1) * MXU_N] = _dot(yb, w_ref[:, lo:lo + MXU_N]).astype(BF16)
    for c in range(q_w // MXU_N):
        lo = q_w + 2 * kv_w + c * MXU_N
        gate_ref[:, c * MXU_N:(c + 1) * MXU_N] = _sigmoid(_dot(yb, w_ref[:, lo:lo + MXU_N])).astype(BF16)


def _attn_inproj(h, gain, w_in, q_gain, k_gain, rope, seq):
    t, d = h.shape
    tm = TM_PROJ
    n_in = w_in.shape[1]
    q_w, kv_w = ATT_HEADS * HEAD_DIM, ATT_KV_HEADS * HEAD_DIM
    seq_tiles = seq // tm
    row = lambda i: (i, 0)
    pos = lambda i: (i % seq_tiles, 0)
    return pl.pallas_call(
        _attn_inproj_kernel,
        grid=(t // tm,),
        in_specs=[
            pl.BlockSpec((tm, d), row),
            _const_spec((1, d)),
            _const_spec((d, n_in)),
            _const_spec((1, HEAD_DIM)),
            _const_spec((1, HEAD_DIM)),
            pl.BlockSpec((tm, HEAD_DIM), pos),
            pl.BlockSpec((tm, HEAD_DIM), pos),
            pl.BlockSpec((tm, HEAD_DIM), pos),
        ],
        out_specs=[
            pl.BlockSpec((tm, q_w), row),
            pl.BlockSpec((tm, kv_w), row),
            pl.BlockSpec((tm, kv_w), row),
            pl.BlockSpec((tm, q_w), row),
        ],
        out_shape=[
            jax.ShapeDtypeStruct((t, q_w), BF16),
            jax.ShapeDtypeStruct((t, kv_w), BF16),
            jax.ShapeDtypeStruct((t, kv_w), BF16),
            jax.ShapeDtypeStruct((t, q_w), BF16),
        ],
        compiler_params=_params("parallel"),
        name="attn_inproj",
    )(h, gain, w_in, q_gain, k_gain, *rope)


def _flash_kernel(shift_ref, q_ref, k_ref, vt_ref, gate_ref, o_ref, qt_ref, m_ref, l_ref, acc_ref, *, seq):
    hd = HEAD_DIM
    group = q_ref.shape[1] // hd
    tq = q_ref.shape[0]
    for g in range(group):
        qt_ref[g] = q_ref[:, g * hd:(g + 1) * hd].astype(F32).T.astype(BF16)
    acc_ref[...] = jnp.zeros(acc_ref.shape, F32)
    l_ref[...] = jnp.zeros(l_ref.shape, F32)

    def sweep(update, tk, unroll, lookahead):
        def scores(unit):
            start, g = unit
            return _dot(k_ref[pl.ds(start, tk), :], qt_ref[g])

        def step(j, carry):
            units = []
            for u in range(unroll):
                start = pl.multiple_of((j * unroll + u) * tk, tk)
                units += [(start, g) for g in range(group)]
            pending = [scores(unit) for unit in units[:lookahead]]
            for idx, (start, g) in enumerate(units):
                s = pending.pop(0)
                if idx + lookahead < len(units):
                    pending.append(scores(units[idx + lookahead]))
                update(g, s, vt_ref[:, pl.ds(start, tk)], tk)
            return carry

        lax.fori_loop(0, seq // (tk * unroll), step, 0)

    def column_sums(p, tk):
        return jnp.sum(p.reshape(tk // HALO, HALO, tq), axis=0)

    def bounded_update(g, s, vt, tk):
        p = jnp.exp2(s - shift_ref[0])
        l_ref[g] += column_sums(p, tk)
        acc_ref[g] += _dot(vt, p.astype(BF16))

    def online_update(g, s, vt, tk):
        m_prev = m_ref[g]
        m_new = jnp.maximum(m_prev, jnp.max(s, axis=0, keepdims=True))
        alpha = jnp.exp2(m_prev - m_new)
        p = jnp.exp2(s - m_new)
        l_ref[g] = l_ref[g] * alpha + column_sums(p, tk)
        acc_ref[g] = acc_ref[g] * alpha + _dot(vt, p.astype(BF16))
        m_ref[g] = m_new

    bounded = shift_ref[1] > 0.5

    @pl.when(bounded)
    def _():
        sweep(bounded_update, *ATT_BOUNDED_UNITS)

    @pl.when(jnp.logical_not(bounded))
    def _():
        m_ref[...] = jnp.full(m_ref.shape, -jnp.inf, F32)
        sweep(online_update, *ATT_ONLINE_UNITS)

    for g in range(group):
        o = (acc_ref[g] / jnp.sum(l_ref[g], axis=0, keepdims=True)).T
        o_ref[:, g * hd:(g + 1) * hd] = (o * gate_ref[:, g * hd:(g + 1) * hd].astype(F32)).astype(BF16)


def _flash_attention(shift, q, k, vt, gate, batch, seq):
    t = q.shape[0]
    hd = HEAD_DIM
    group = ATT_HEADS // ATT_KV_HEADS
    tq = ATT_TQ
    nq = seq // tq
    qmap = lambda b, g, i: (b * nq + i, g)
    return pl.pallas_call(
        functools.partial(_flash_kernel, seq=seq),
        grid=(batch, ATT_KV_HEADS, nq),
        in_specs=[
            pl.BlockSpec(memory_space=pltpu.SMEM),
            pl.BlockSpec((tq, group * hd), qmap),
            pl.BlockSpec((seq, hd), lambda b, g, i: (b, g)),
            pl.BlockSpec((None, hd, seq), lambda b, g, i: (b * ATT_KV_HEADS + g, 0, 0)),
            pl.BlockSpec((tq, group * hd), qmap),
        ],
        out_specs=pl.BlockSpec((tq, group * hd), qmap),
        out_shape=jax.ShapeDtypeStruct((t, ATT_HEADS * hd), BF16),
        scratch_shapes=[
            pltpu.VMEM((group, hd, tq), BF16),
            pltpu.VMEM((group, 1, tq), F32),
            pltpu.VMEM((group, HALO, tq), F32),
            pltpu.VMEM((group, hd, tq), F32),
        ],
        compiler_params=_params("parallel", "parallel", "parallel"),
        name="flash_attention",
    )(shift, q, k, vt, gate)


def _outproj_kernel(a_ref, w_ref, res_ref, o_ref):
    o_ref[...] = res_ref[...] + _dot(a_ref[...], w_ref[...])


def _outproj(a, w, res):
    t, k = a.shape
    d = w.shape[1]
    tm = TM_PROJ
    row = lambda i: (i, 0)
    return pl.pallas_call(
        _outproj_kernel,
        grid=(t // tm,),
        in_specs=[pl.BlockSpec((tm, k), row), _const_spec((k, d)), pl.BlockSpec((tm, d), row)],
        out_specs=pl.BlockSpec((tm, d), row),
        out_shape=jax.ShapeDtypeStruct((t, d), F32),
        compiler_params=_params("parallel"),
        name="attn_outproj",
    )(a, w, res)


def _ffn_kernel(x_ref, xp_ref, xn_ref, g_ref, wup_ref, cw_ref, cb_ref, wdn_ref, gf_ref, o_ref, act_ref,
                *, seq_tiles, final_norm):
    tm = x_ref.shape[0]
    d_ff = wdn_ref.shape[0]
    y_ext = _normed_with_halo(x_ref, xp_ref, xn_ref, g_ref[...], seq_tiles)
    for c in range(d_ff // MXU_N):
        ca = slice(c * MXU_N, (c + 1) * MXU_N)
        cb = slice(d_ff + c * MXU_N, d_ff + (c + 1) * MXU_N)
        a = _seq_conv(_dot(y_ext, wup_ref[:, ca]), cw_ref, cb_ref, ca, tm)
        b = _seq_conv(_dot(y_ext, wup_ref[:, cb]), cw_ref, cb_ref, cb, tm)
        act_ref[:, ca] = (a * _sigmoid(a) * b).astype(BF16)
    out = x_ref[...] + _dot(act_ref[...], wdn_ref[...])
    if final_norm:
        out = _rms(out, gf_ref[...])
    o_ref[...] = out


def _ffn(h, gain, w_up, conv_w, conv_b, w_down, final_gain, seq, layer, final_norm):
    t, d = h.shape
    tm = TM_PROJ
    n_tiles = t // tm
    return pl.pallas_call(
        functools.partial(_ffn_kernel, seq_tiles=seq // tm, final_norm=final_norm),
        grid=(n_tiles,),
        in_specs=_halo_specs(tm, d, n_tiles) + [
            _const_spec((1, d)),
            _layer_spec(w_up.shape, layer),
            _layer_spec(conv_w.shape, layer),
            _layer_spec(conv_b.shape, layer),
            _layer_spec(w_down.shape, layer),
            _const_spec((1, d)),
        ],
        out_specs=pl.BlockSpec((tm, d), lambda i: (i, 0)),
        out_shape=jax.ShapeDtypeStruct((t, d), F32),
        scratch_shapes=[pltpu.VMEM((tm, w_down.shape[1]), BF16)],
        compiler_params=_params("parallel"),
        name="conv_ffn",
    )(h, h, h, gain, w_up, conv_w, conv_b, w_down, final_gain)


def _gdn_inproj_kernel(x_ref, xp_ref, xn_ref, g_ref, w_ref, cw_ref, cb_ref, alog_ref, dtb_ref, lower_ref, upper_ref,
                       q_ref, k_ref, v_ref, z_ref, gates_ref, *, seq_tiles):
    hd = HEAD_DIM
    tm = x_ref.shape[0]
    nh = GDN_V_HEADS
    qk_w, v_w = GDN_QK_HEADS * hd, GDN_V_HEADS * hd
    conv_w = 2 * qk_w + v_w
    y_ext = _normed_with_halo(x_ref, xp_ref, xn_ref, g_ref[...], seq_tiles)
    y = y_ext[0:tm]

    def l2n(z, scale=1.0):
        return z * (lax.rsqrt(jnp.sum(z * z, axis=-1, keepdims=True) + EPS) * scale)

    for c in range(conv_w // MXU_N):
        cols = slice(c * MXU_N, (c + 1) * MXU_N)
        u = _seq_conv(_dot(y_ext, w_ref[:, cols]), cw_ref, cb_ref, cols, tm)
        u = u * _sigmoid(u)
        lo = c * MXU_N
        if lo < qk_w:
            for e in range(MXU_N // hd):
                q_ref[:, lo + e * hd:lo + (e + 1) * hd] = l2n(u[:, e * hd:(e + 1) * hd], hd ** -0.5).astype(BF16)
        elif lo < 2 * qk_w:
            for e in range(MXU_N // hd):
                k_ref[:, lo - qk_w + e * hd:lo - qk_w + (e + 1) * hd] = l2n(u[:, e * hd:(e + 1) * hd]).astype(BF16)
        else:
            v_ref[:, lo - 2 * qk_w:lo - 2 * qk_w + MXU_N] = u.astype(BF16)
    for c in range(v_w // MXU_N):
        cols = slice(c * MXU_N, (c + 1) * MXU_N)
        z_ref[:, cols] = _dot(y, w_ref[:, conv_w + c * MXU_N:conv_w + (c + 1) * MXU_N]).astype(BF16)

    ab = _dot(y, w_ref[:, conv_w + v_w:conv_w + v_w + 4 * nh])
    ab = jnp.concatenate([ab, jnp.zeros((tm, hd - 4 * nh), F32)], axis=1)
    lane = lax.broadcasted_iota(jnp.int32, ab.shape, 1)
    xa = ab + dtb_ref[...]
    softplus = jnp.maximum(xa, 0.0) + jnp.log(1.0 + jnp.exp(-jnp.abs(xa)))
    g = -jnp.exp(alog_ref[...]) * softplus
    beta = _sigmoid(ab)
    g1 = g.astype(BF16)
    r1 = g - g1.astype(F32)
    g2 = r1.astype(BF16)
    g3 = (r1 - g2.astype(F32)).astype(BF16)
    gs = jnp.concatenate([g1, g2, g3], axis=1)
    cf = _dot(lower_ref[...], gs)
    cb = _dot(upper_ref[...], gs)
    cum_f = cf[:, 0:hd] + cf[:, hd:2 * hd] + cf[:, 2 * hd:3 * hd]
    cum_b = cb[:, 0:hd] + cb[:, hd:2 * hd] + cb[:, 2 * hd:3 * hd]
    gates_ref[...] = jnp.where(lane < nh, cum_f, jnp.where((lane >= 2 * nh) & (lane < 3 * nh), cum_b, beta))


def _gdn_inproj(h, gain, w_in, conv_w, conv_b, alog_row, dtb_row, seq):
    t, d = h.shape
    tm = TM_GDN_IN
    hd = HEAD_DIM
    qk_w, v_w = GDN_QK_HEADS * hd, GDN_V_HEADS * hd
    n_tiles = t // tm
    row = lambda i: (i, 0)
    r = lax.broadcasted_iota(jnp.int32, (tm, tm), 0)
    c = lax.broadcasted_iota(jnp.int32, (tm, tm), 1)
    same = (r // GDN_CHUNK) == (c // GDN_CHUNK)
    lower = (same & (c <= r)).astype(BF16)
    upper = (same & (c >= r)).astype(BF16)
    return pl.pallas_call(
        functools.partial(_gdn_inproj_kernel, seq_tiles=seq // tm),
        grid=(n_tiles,),
        in_specs=_halo_specs(tm, d, n_tiles) + [
            _const_spec((1, d)),
            _const_spec(w_in.shape),
            _const_spec(conv_w.shape),
            _const_spec(conv_b.shape),
            _const_spec((1, hd)),
            _const_spec((1, hd)),
            _const_spec((tm, tm)),
            _const_spec((tm, tm)),
        ],
        out_specs=[
            pl.BlockSpec((tm, qk_w), row),
            pl.BlockSpec((tm, qk_w), row),
            pl.BlockSpec((tm, v_w), row),
            pl.BlockSpec((tm, v_w), row),
            pl.BlockSpec((tm, hd), row),
        ],
        out_shape=[
            jax.ShapeDtypeStruct((t, qk_w), BF16),
            jax.ShapeDtypeStruct((t, qk_w), BF16),
            jax.ShapeDtypeStruct((t, v_w), BF16),
            jax.ShapeDtypeStruct((t, v_w), BF16),
            jax.ShapeDtypeStruct((t, hd), F32),
        ],
        compiler_params=_params("parallel"),
        name="gdn_inproj",
    )(h, h, h, gain, w_in, conv_w, conv_b, alog_row, dtb_row, lower, upper)


def _unit_triangular_inverses(l_mats, row, col):
    n = l_mats[0].shape[0]
    size = GDN_DIAG
    same = (row // size) == (col // size)
    eye = jnp.where(row == col, 1.0, 0.0)
    lds = [jnp.where(same, l, 0.0) for l in l_mats]
    ldbs = [ld.astype(BF16) for ld in lds]
    qs = [_dot(ldb, ldb) for ldb in ldbs]
    ps = [eye - ld for ld in lds]
    power = 2
    while 2 * power < size:
        qbs = [q.astype(BF16) for q in qs]
        xs = [_dot(qb, jnp.concatenate([qb, p.astype(BF16)], axis=1)) for qb, p in zip(qbs, ps)]
        qs = [x[:, 0:n] for x in xs]
        ps = [p + x[:, n:2 * n] for p, x in zip(ps, xs)]
        power *= 2
    ds = [p + _dot(q.astype(BF16), p.astype(BF16)) for q, p in zip(qs, ps)]
    while size < n:
        pair = (row // (2 * size)) == (col // (2 * size))
        join = pair & jnp.logical_not(same)
        dbs = [d.astype(BF16) for d in ds]
        es = [_dot(jnp.where(join, l, 0.0).astype(BF16), db).astype(BF16) for l, db in zip(l_mats, dbs)]
        ds = [d - _dot(db, e) for d, db, e in zip(ds, dbs, es)]
        same = pair
        size *= 2
    return ds


def _gdn_kernel(q_ref, k_ref, v_ref, gates_ref, gates_t_ref, o_ref, state_ref, *, reverse):
    hd = HEAD_DIM
    n = GDN_CHUNK
    nh = GDN_V_HEADS
    rep = GDN_V_HEADS // GDN_QK_HEADS

    @pl.when(pl.program_id(1) == 0)
    def _():
        state_ref[...] = jnp.zeros(state_ref.shape, F32)

    row = lax.broadcasted_iota(jnp.int32, (n, n), 0)
    col = lax.broadcasted_iota(jnp.int32, (n, n), 1)
    incl = (col >= row) if reverse else (col <= row)
    strict = (col > row) if reverse else (col < row)
    last = 0 if reverse else n - 1
    gc_lane = 2 * nh if reverse else 0
    beta_lane = 3 * nh if reverse else nh
    gates = gates_ref[...]

    for first in range(0, nh, GDN_HEAD_GROUP):
        heads = list(range(first, first + GDN_HEAD_GROUP))
        qk_heads = sorted({h // rep for h in heads})
        kq = {hq: k_ref[:, hq * hd:(hq + 1) * hd] for hq in qk_heads}
        qq = {hq: q_ref[:, hq * hd:(hq + 1) * hd] for hq in qk_heads}
        kk = {hq: _dot_nt(kq[hq], kq[hq]) for hq in qk_heads}
        qk = {hq: _dot_nt(qq[hq], kq[hq]) for hq in qk_heads}
        gcol, bcol, g_last, decay, l_mats = [], [], [], [], []
        for h in heads:
            gc = jnp.broadcast_to(gates[:, gc_lane + h:gc_lane + h + 1], (n, n))
            bc = jnp.broadcast_to(gates[:, beta_lane + h:beta_lane + h + 1], (n, n))
            grow = gates_t_ref[gc_lane + h:gc_lane + h + 1, :]
            dec = jnp.exp(jnp.where(incl, gc - grow, -1e30))
            gcol.append(gc)
            bcol.append(bc)
            g_last.append(grow[:, last:last + 1])
            decay.append(dec)
            l_mats.append(jnp.where(strict, kk[h // rep] * dec * bc, 0.0))
        t_inv = _unit_triangular_inverses(l_mats, row, col)
        uw, lhs_state, lhs_new = [], [], []
        for i, h in enumerate(heads):
            kf = kq[h // rep].astype(F32)
            qf = qq[h // rep].astype(F32)
            eg = jnp.exp(gcol[i])
            vf = v_ref[:, h * hd:(h + 1) * hd].astype(F32)
            rhs = jnp.concatenate([(vf * bcol[i]).astype(BF16), (kf * bcol[i] * eg).astype(BF16)], axis=1)
            uw.append(_dot(t_inv[i].astype(BF16), rhs))
            attn = (qk[h // rep] * decay[i]).astype(BF16)
            kd_t = (kf * jnp.exp(g_last[i] - gcol[i])).T.astype(BF16)
            lhs_state.append((qf * eg).astype(BF16))
            lhs_new.append(jnp.concatenate([attn, kd_t], axis=0))
        states = [state_ref[h] for h in heads]
        pq = [_dot(jnp.concatenate([x[:, hd:2 * hd].astype(BF16), qd], axis=0), s.astype(BF16))
              for x, qd, s in zip(uw, lhs_state, states)]
        v_new = [(x[:, 0:hd] - y[0:n]).astype(BF16) for x, y in zip(uw, pq)]
        r = [_dot(a, b) for a, b in zip(lhs_new, v_new)]
        for i, h in enumerate(heads):
            o_ref[:, h * hd:(h + 1) * hd] = pq[i][n:2 * n] + r[i][0:n]
            state_ref[h] = states[i] * jnp.exp(g_last[i]) + r[i][n:2 * n]


def _gdn_scan(q, k, v, gates, gates_t, batch, seq, reverse):
    t = q.shape[0]
    hd = HEAD_DIM
    n = GDN_CHUNK
    nc = seq // n
    qk_w, v_w = GDN_QK_HEADS * hd, GDN_V_HEADS * hd
    blk = (lambda b, c: b * nc + nc - 1 - c) if reverse else (lambda b, c: b * nc + c)
    row = lambda b, c: (blk(b, c), 0)
    return pl.pallas_call(
        functools.partial(_gdn_kernel, reverse=reverse),
        grid=(batch, nc),
        in_specs=[
            pl.BlockSpec((n, qk_w), row),
            pl.BlockSpec((n, qk_w), row),
            pl.BlockSpec((n, v_w), row),
            pl.BlockSpec((n, hd), row),
            pl.BlockSpec((hd, n), lambda b, c: (0, blk(b, c))),
        ],
        out_specs=pl.BlockSpec((n, v_w), row),
        out_shape=jax.ShapeDtypeStruct((t, v_w), F32),
        scratch_shapes=[pltpu.VMEM((GDN_V_HEADS, hd, hd), F32)],
        compiler_params=_params("parallel", "arbitrary"),
        name="gdn_scan_bwd" if reverse else "gdn_scan_fwd",
    )(q, k, v, gates, gates_t)


def _gdn_outproj_kernel(of_ref, ob_ref, z_ref, g_ref, w_ref, res_ref, o_ref, act_ref):
    hd = HEAD_DIM
    for h in range(GDN_V_HEADS):
        cols = slice(h * hd, (h + 1) * hd)
        o = _rms(of_ref[:, cols] + ob_ref[:, cols], g_ref[...])
        z = z_ref[:, cols].astype(F32)
        act_ref[:, cols] = (o * (z * _sigmoid(z))).astype(BF16)
    o_ref[...] = res_ref[...] + _dot(act_ref[...], w_ref[...])


def _gdn_outproj(o_f, o_b, z, gain, w, res):
    t, k = o_f.shape
    d = w.shape[1]
    tm = TM_PROJ
    row = lambda i: (i, 0)
    return pl.pallas_call(
        _gdn_outproj_kernel,
        grid=(t // tm,),
        in_specs=[
            pl.BlockSpec((tm, k), row),
            pl.BlockSpec((tm, k), row),
            pl.BlockSpec((tm, k), row),
            _const_spec((1, HEAD_DIM)),
            _const_spec((k, d)),
            pl.BlockSpec((tm, d), row),
        ],
        out_specs=pl.BlockSpec((tm, d), row),
        out_shape=jax.ShapeDtypeStruct((t, d), F32),
        scratch_shapes=[pltpu.VMEM((tm, k), BF16)],
        compiler_params=_params("parallel"),
        name="gdn_outproj",
    )(o_f, o_b, z, gain, w, res)


def _rope_tables(seq):
    grid_rows = seq // ROPE_GRID_W
    half = HEAD_DIM // 2
    n_freq = half // 2
    inv_freq = ROPE_THETA ** (-(jnp.arange(n_freq, dtype=F32) * 2.0 / half))
    ang_r = jnp.arange(grid_rows, dtype=F32)[:, None] * inv_freq[None, :]
    ang_c = jnp.arange(ROPE_GRID_W, dtype=F32)[:, None] * inv_freq[None, :]
    cos_r, sin_r, cos_c, sin_c = lax.optimization_barrier(
        (jnp.cos(ang_r), jnp.sin(ang_r), jnp.cos(ang_c), jnp.sin(ang_c)))
    by_row = lambda a: jnp.repeat(a, ROPE_GRID_W, axis=0)
    by_col = lambda a: jnp.tile(a, (grid_rows, 1))
    zero = jnp.zeros((seq, n_freq), F32)
    cos = jnp.concatenate([by_row(cos_r), by_row(cos_r), by_col(cos_c), by_col(cos_c)], axis=-1)
    sa = jnp.concatenate([-by_row(sin_r), zero, -by_col(sin_c), zero], axis=-1)
    sb = jnp.concatenate([zero, by_row(sin_r), zero, by_col(sin_c)], axis=-1)
    return cos, sa, sb


def kernel(x, norm_mix, norm_ffn, norm_final, attn_w_in, attn_q_norm, attn_k_norm, attn_w_out, gdn_w_in, gdn_conv_w, gdn_conv_b, gdn_a_log, gdn_dt_bias, gdn_o_norm, gdn_w_out, ffn_w_up, ffn_conv_w, ffn_conv_b, ffn_w_down):
    batch, seq, d = x.shape
    t = batch * seq
    hd = HEAD_DIM
    assert seq % TM_PROJ == 0 and seq % GDN_CHUNK == 0
    assert all(seq % (tk * unroll) == 0 for tk, unroll, _ in (ATT_BOUNDED_UNITS, ATT_ONLINE_UNITS))
    assert seq % ROPE_GRID_W == 0
    h = x.reshape(t, d)
    row = lambda a: a.reshape(1, -1)
    final_gain = row(norm_final)

    rope = _rope_tables(seq)
    q_gain = row(attn_q_norm[0]) * (hd ** -0.5 * LOG2E)
    q, k, v, gate = _attn_inproj(h, row(norm_mix[0]), attn_w_in[0].astype(BF16), q_gain,
                                 row(attn_k_norm[0]), rope, seq)
    vt = v.reshape(batch, seq, ATT_KV_HEADS, hd).transpose(0, 2, 3, 1).reshape(batch * ATT_KV_HEADS, hd, seq)
    bound = hd * jnp.max(jnp.abs(q_gain)) * jnp.max(jnp.abs(attn_k_norm[0])) * ATT_BOUND_SLACK
    shift = jnp.stack([bound, (bound <= ATT_MAX_SHIFT).astype(F32)])
    attn = _flash_attention(shift, q, k, vt, gate, batch, seq)
    h = _outproj(attn, attn_w_out[0].astype(BF16), h)
    ffn = (ffn_w_up.astype(BF16), ffn_conv_w, ffn_conv_b[:, None, :], ffn_w_down.astype(BF16), final_gain, seq)
    h = _ffn(h, row(norm_ffn[0]), *ffn, 0, False)

    pad_gate = lambda p: jnp.pad(jnp.stack([p[0], jnp.zeros_like(p[0]), p[1], jnp.zeros_like(p[1])]).reshape(1, -1),
                                 ((0, 0), (0, hd - 4 * GDN_V_HEADS)))
    gq, gk, gv, gz, gates = _gdn_inproj(h, row(norm_mix[1]), gdn_w_in[0].astype(BF16), gdn_conv_w[0],
                                        row(gdn_conv_b[0]), pad_gate(gdn_a_log[0]), pad_gate(gdn_dt_bias[0]), seq)
    gates_t = gates.T
    o_f = _gdn_scan(gq, gk, gv, gates, gates_t, batch, seq, False)
    o_b = _gdn_scan(gq, gk, gv, gates, gates_t, batch, seq, True)
    h = _gdn_outproj(o_f, o_b, gz, row(gdn_o_norm[0]), gdn_w_out[0].astype(BF16), h)
    h = _ffn(h, row(norm_ffn[1]), *ffn, 1, True)
    return h.reshape(batch, seq, d)
```

```python
import functools

import jax
import jax.numpy as jnp
from jax import lax
from jax.experimental import pallas as pl
from jax.experimental.pallas import tpu as pltpu

F32 = jnp.float32
BF16 = jnp.bfloat16

EPS = 1e-6
ROPE_THETA = 10000.0
ROPE_GRID_W = 64

HEAD_DIM = 128
ATT_HEADS = 8
ATT_KV_HEADS = 2
GDN_QK_HEADS = 8
GDN_V_HEADS = 16
GDN_CHUNK = 128
GDN_DIAG = 16
GDN_HEAD_GROUP = 16

HALO = 8
MXU_N = 256
V7X_VMEM_LIMIT = 56 * 1024 * 1024

TM_PROJ = 512
TM_GDN_IN = 256
ATT_TQ = 256
ATT_BOUNDED_UNITS = (1024, 4, 2)
ATT_ONLINE_UNITS = (512, 8, 4)
ATT_MAX_SHIFT = 50.0
ATT_BOUND_SLACK = 1.02
LOG2E = 1.4426950408889634


def _dot(a, b):
    return jnp.dot(a, b, preferred_element_type=F32)


def _dot_nt(a, b):
    return lax.dot_general(a, b, (((1,), (1,)), ((), ())), preferred_element_type=F32)


def _rms(x, gain):
    return x * lax.rsqrt(jnp.mean(x * x, axis=-1, keepdims=True) + EPS) * gain


def _sigmoid(x):
    return 1.0 / (1.0 + jnp.exp2(x * (-LOG2E)))


def _const_spec(shape):
    return pl.BlockSpec(shape, lambda *_: (0,) * len(shape), pipeline_mode=pl.Buffered(1))


def _layer_spec(shape, layer):
    return pl.BlockSpec((None,) + tuple(shape[1:]), lambda *_: (layer,) + (0,) * (len(shape) - 1),
                        pipeline_mode=pl.Buffered(1))


def _params(*semantics):
    return pltpu.CompilerParams(dimension_semantics=semantics, vmem_limit_bytes=V7X_VMEM_LIMIT)


def _halo_specs(tm, d, n_tiles):
    r = tm // HALO
    last = n_tiles * r - 1
    return [
        pl.BlockSpec((tm, d), lambda i: (i, 0)),
        pl.BlockSpec((HALO, d), lambda i: (jnp.maximum(i * r - 1, 0), 0)),
        pl.BlockSpec((HALO, d), lambda i: (jnp.minimum((i + 1) * r, last), 0)),
    ]


def _normed_with_halo(x_ref, xp_ref, xn_ref, gain, seq_tiles):
    i = pl.program_id(0)
    pos = i % seq_tiles
    y = _rms(x_ref[...], gain)
    yn = jnp.where(pos == seq_tiles - 1, 0.0, _rms(xn_ref[...], gain))
    yp = jnp.where(pos == 0, 0.0, _rms(xp_ref[...], gain))
    return jnp.concatenate([y, yn, yp], axis=0).astype(BF16)


def _shift_rows(u, k):
    rows, cols = u.shape
    nb = rows // HALO
    blocks = u.reshape(nb, HALO, cols)
    sub = lax.broadcasted_iota(jnp.int32, (1, HALO, cols), 1)
    if k > 0:
        mixed = jnp.where(sub >= HALO - k, jnp.concatenate([blocks[nb - 1:], blocks[:nb - 1]], axis=0), blocks)
    else:
        mixed = jnp.where(sub < -k, jnp.concatenate([blocks[1:], blocks[:1]], axis=0), blocks)
    return pltpu.roll(mixed, k % HALO, axis=1).reshape(rows, cols)


def _seq_conv(u, w_ref, b_ref, cols, tm):
    taps = w_ref.shape[0]
    half = taps // 2
    acc = u * w_ref[half:half + 1, cols]
    for t in range(taps):
        if t != half:
            acc = acc + _shift_rows(u, half - t) * w_ref[t:t + 1, cols]
    return acc[0:tm] + b_ref[:, cols]


def _attn_inproj_kernel(x_ref, g_ref, w_ref, qg_ref, kg_ref, cos_ref, sa_ref, sb_ref,
                        q_ref, k_ref, v_ref, gate_ref):
    hd = HEAD_DIM
    yb = _rms(x_ref[...], g_ref[...]).astype(BF16)
    cos, sa, sb = cos_ref[...], sa_ref[...], sb_ref[...]
    q_w = ATT_HEADS * hd
    kv_w = ATT_KV_HEADS * hd

    def norm_rope(z, gain):
        zn = _rms(z, gain)
        return zn * cos + pltpu.roll(zn, hd - hd // 4, 1) * sa + pltpu.roll(zn, hd // 4, 1) * sb

    for c in range(q_w // MXU_N):
        z = _dot(yb, w_ref[:, c * MXU_N:(c + 1) * MXU_N])
        for e in range(MXU_N // hd):
            col = c * MXU_N + e * hd
            q_ref[:, col:col + hd] = norm_rope(z[:, e * hd:(e + 1) * hd], qg_ref[...]).astype(BF16)
    for c in range(kv_w // MXU_N):
        z = _dot(yb, w_ref[:, q_w + c * MXU_N:q_w + (c + 1) * MXU_N])
        for e in range(MXU_N // hd):
            col = c * MXU_N + e * hd
            k_ref[:, col:col + hd] = norm_rope(z[:, e * hd:(e + 1) * hd], kg_ref[...]).astype(BF16)
    for c in range(kv_w // MXU_N):
        lo = q_w + kv_w + c * MXU_N
        v_ref[:, c * MXU_N:(c + 1) * MXU_N] = _dot(yb, w_ref[:, lo:lo + MXU_N]).astype(BF16)
    for c in range(q_w // MXU_N):
        lo = q_w + 2 * kv_w + c * MXU_N
        gate_ref[:, c * MXU_N:(c + 1) * MXU_N] = _sigmoid(_dot(yb, w_ref[:, lo:lo + MXU_N])).astype(BF16)


def _attn_inproj(h, gain, w_in, q_gain, k_gain, rope, seq):
    t, d = h.shape
    tm = TM_PROJ
    n_in = w_in.shape[1]
    q_w, kv_w = ATT_HEADS * HEAD_DIM, ATT_KV_HEADS * HEAD_DIM
    seq_tiles = seq // tm
    row = lambda i: (i, 0)
    pos = lambda i: (i % seq_tiles, 0)
    return pl.pallas_call(
        _attn_inproj_kernel,
        grid=(t // tm,),
        in_specs=[
            pl.BlockSpec((tm, d), row),
            _const_spec((1, d)),
            _const_spec((d, n_in)),
            _const_spec((1, HEAD_DIM)),
            _const_spec((1, HEAD_DIM)),
            pl.BlockSpec((tm, HEAD_DIM), pos),
            pl.BlockSpec((tm, HEAD_DIM), pos),
            pl.BlockSpec((tm, HEAD_DIM), pos),
        ],
        out_specs=[
            pl.BlockSpec((tm, q_w), row),
            pl.BlockSpec((tm, kv_w), row),
            pl.BlockSpec((tm, kv_w), row),
            pl.BlockSpec((tm, q_w), row),
        ],
        out_shape=[
            jax.ShapeDtypeStruct((t, q_w), BF16),
            jax.ShapeDtypeStruct((t, kv_w), BF16),
            jax.ShapeDtypeStruct((t, kv_w), BF16),
            jax.ShapeDtypeStruct((t, q_w), BF16),
        ],
        compiler_params=_params("parallel"),
        name="attn_inproj",
    )(h, gain, w_in, q_gain, k_gain, *rope)


def _flash_kernel(shift_ref, q_ref, k_ref, vt_ref, gate_ref, o_ref, qt_ref, m_ref, l_ref, acc_ref, *, seq):
    hd = HEAD_DIM
    group = q_ref.shape[1] // hd
    tq = q_ref.shape[0]
    for g in range(group):
        qt_ref[g] = q_ref[:, g * hd:(g + 1) * hd].astype(F32).T.astype(BF16)
    acc_ref[...] = jnp.zeros(acc_ref.shape, F32)
    l_ref[...] = jnp.zeros(l_ref.shape, F32)

    def sweep(update, tk, unroll, lookahead):
        def scores(unit):
            start, g = unit
            return _dot(k_ref[pl.ds(start, tk), :], qt_ref[g])

        def step(j, carry):
            units = []
            for u in range(unroll):
                start = pl.multiple_of((j * unroll + u) * tk, tk)
                units += [(start, g) for g in range(group)]
            pending = [scores(unit) for unit in units[:lookahead]]
            for idx, (start, g) in enumerate(units):
                s = pending.pop(0)
                if idx + lookahead < len(units):
                    pending.append(scores(units[idx + lookahead]))
                update(g, s, vt_ref[:, pl.ds(start, tk)], tk)
            return carry

        lax.fori_loop(0, seq // (tk * unroll), step, 0)

    def column_sums(p, tk):
        return jnp.sum(p.reshape(tk // HALO, HALO, tq), axis=0)

    def bounded_update(g, s, vt, tk):
        p = jnp.exp2(s - shift_ref[0])
        l_ref[g] += column_sums(p, tk)
        acc_ref[g] += _dot(vt, p.astype(BF16))

    def online_update(g, s, vt, tk):
        m_prev = m_ref[g]
        m_new = jnp.maximum(m_prev, jnp.max(s, axis=0, keepdims=True))
        alpha = jnp.exp2(m_prev - m_new)
        p = jnp.exp2(s - m_new)
        l_ref[g] = l_ref[g] * alpha + column_sums(p, tk)
        acc_ref[g] = acc_ref[g] * alpha + _dot(vt, p.astype(BF16))
        m_ref[g] = m_new

    bounded = shift_ref[1] > 0.5

    @pl.when(bounded)
    def _():
        sweep(bounded_update, *ATT_BOUNDED_UNITS)

    @pl.when(jnp.logical_not(bounded))
    def _():
        m_ref[...] = jnp.full(m_ref.shape, -jnp.inf, F32)
        sweep(online_update, *ATT_ONLINE_UNITS)

    for g in range(group):
        o = (acc_ref[g] / jnp.sum(l_ref[g], axis=0, keepdims=True)).T
        o_ref[:, g * hd:(g + 1) * hd] = (o * gate_ref[:, g * hd:(g + 1) * hd].astype(F32)).astype(BF16)


def _flash_attention(shift, q, k, vt, gate, batch, seq):
    t = q.shape[0]
    hd = HEAD_DIM
    group = ATT_HEADS // ATT_KV_HEADS
    tq = ATT_TQ
    nq = seq // tq
    qmap = lambda b, g, i: (b * nq + i, g)
    return pl.pallas_call(
        functools.partial(_flash_kernel, seq=seq),
        grid=(batch, ATT_KV_HEADS, nq),
        in_specs=[
            pl.BlockSpec(memory_space=pltpu.SMEM),
            pl.BlockSpec((tq, group * hd), qmap),
            pl.BlockSpec((seq, hd), lambda b, g, i: (b, g)),
            pl.BlockSpec((None, hd, seq), lambda b, g, i: (b * ATT_KV_HEADS + g, 0, 0)),
            pl.BlockSpec((tq, group * hd), qmap),
        ],
        out_specs=pl.BlockSpec((tq, group * hd), qmap),
        out_shape=jax.ShapeDtypeStruct((t, ATT_HEADS * hd), BF16),
        scratch_shapes=[
            pltpu.VMEM((group, hd, tq), BF16),
            pltpu.VMEM((group, 1, tq), F32),
            pltpu.VMEM((group, HALO, tq), F32),
            pltpu.VMEM((group, hd, tq), F32),
        ],
        compiler_params=_params("parallel", "parallel", "parallel"),
        name="flash_attention",
    )(shift, q, k, vt, gate)


def _outproj_kernel(a_ref, w_ref, res_ref, o_ref):
    o_ref[...] = res_ref[...] + _dot(a_ref[...], w_ref[...])


def _outproj(a, w, res):
    t, k = a.shape
    d = w.shape[1]
    tm = TM_PROJ
    row = lambda i: (i, 0)
    return pl.pallas_call(
        _outproj_kernel,
        grid=(t // tm,),
        in_specs=[pl.BlockSpec((tm, k), row), _const_spec((k, d)), pl.BlockSpec((tm, d), row)],
        out_specs=pl.BlockSpec((tm, d), row),
        out_shape=jax.ShapeDtypeStruct((t, d), F32),
        compiler_params=_params("parallel"),
        name="attn_outproj",
    )(a, w, res)


def _ffn_kernel(x_ref, xp_ref, xn_ref, g_ref, wup_ref, cw_ref, cb_ref, wdn_ref, gf_ref, o_ref, act_ref,
                *, seq_tiles, final_norm):
    tm = x_ref.shape[0]
    d_ff = wdn_ref.shape[0]
    y_ext = _normed_with_halo(x_ref, xp_ref, xn_ref, g_ref[...], seq_tiles)
    for c in range(d_ff // MXU_N):
        ca = slice(c * MXU_N, (c + 1) * MXU_N)
        cb = slice(d_ff + c * MXU_N, d_ff + (c + 1) * MXU_N)
        a = _seq_conv(_dot(y_ext, wup_ref[:, ca]), cw_ref, cb_ref, ca, tm)
        b = _seq_conv(_dot(y_ext, wup_ref[:, cb]), cw_ref, cb_ref, cb, tm)
        act_ref[:, ca] = (a * _sigmoid(a) * b).astype(BF16)
    out = x_ref[...] + _dot(act_ref[...], wdn_ref[...])
    if final_norm:
        out = _rms(out, gf_ref[...])
    o_ref[...] = out


def _ffn(h, gain, w_up, conv_w, conv_b, w_down, final_gain, seq, layer, final_norm):
    t, d = h.shape
    tm = TM_PROJ
    n_tiles = t // tm
    return pl.pallas_call(
        functools.partial(_ffn_kernel, seq_tiles=seq // tm, final_norm=final_norm),
        grid=(n_tiles,),
        in_specs=_halo_specs(tm, d, n_tiles) + [
            _const_spec((1, d)),
            _layer_spec(w_up.shape, layer),
            _layer_spec(conv_w.shape, layer),
            _layer_spec(conv_b.shape, layer),
            _layer_spec(w_down.shape, layer),
            _const_spec((1, d)),
        ],
        out_specs=pl.BlockSpec((tm, d), lambda i: (i, 0)),
        out_shape=jax.ShapeDtypeStruct((t, d), F32),
        scratch_shapes=[pltpu.VMEM((tm, w_down.shape[1]), BF16)],
        compiler_params=_params("parallel"),
        name="conv_ffn",
    )(h, h, h, gain, w_up, conv_w, conv_b, w_down, final_gain)


def _gdn_inproj_kernel(x_ref, xp_ref, xn_ref, g_ref, w_ref, cw_ref, cb_ref, alog_ref, dtb_ref, lower_ref, upper_ref,
                       q_ref, k_ref, v_ref, z_ref, gates_ref, *, seq_tiles):
    hd = HEAD_DIM
    tm = x_ref.shape[0]
    nh = GDN_V_HEADS
    qk_w, v_w = GDN_QK_HEADS * hd, GDN_V_HEADS * hd
    conv_w = 2 * qk_w + v_w
    y_ext = _normed_with_halo(x_ref, xp_ref, xn_ref, g_ref[...], seq_tiles)
    y = y_ext[0:tm]

    def l2n(z, scale=1.0):
        return z * (lax.rsqrt(jnp.sum(z * z, axis=-1, keepdims=True) + EPS) * scale)

    for c in range(conv_w // MXU_N):
        cols = slice(c * MXU_N, (c + 1) * MXU_N)
        u = _seq_conv(_dot(y_ext, w_ref[:, cols]), cw_ref, cb_ref, cols, tm)
        u = u * _sigmoid(u)
        lo = c * MXU_N
        if lo < qk_w:
            for e in range(MXU_N // hd):
                q_ref[:, lo + e * hd:lo + (e + 1) * hd] = l2n(u[:, e * hd:(e + 1) * hd], hd ** -0.5).astype(BF16)
        elif lo < 2 * qk_w:
            for e in range(MXU_N // hd):
                k_ref[:, lo - qk_w + e * hd:lo - qk_w + (e + 1) * hd] = l2n(u[:, e * hd:(e + 1) * hd]).astype(BF16)
        else:
            v_ref[:, lo - 2 * qk_w:lo - 2 * qk_w + MXU_N] = u.astype(BF16)
    for c in range(v_w // MXU_N):
        cols = slice(c * MXU_N, (c + 1) * MXU_N)
        z_ref[:, cols] = _dot(y, w_ref[:, conv_w + c * MXU_N:conv_w + (c + 1) * MXU_N]).astype(BF16)

    ab = _dot(y, w_ref[:, conv_w + v_w:conv_w + v_w + 4 * nh])
    ab = jnp.concatenate([ab, jnp.zeros((tm, hd - 4 * nh), F32)], axis=1)
    lane = lax.broadcasted_iota(jnp.int32, ab.shape, 1)
    xa = ab + dtb_ref[...]
    softplus = jnp.maximum(xa, 0.0) + jnp.log(1.0 + jnp.exp(-jnp.abs(xa)))
    g = -jnp.exp(alog_ref[...]) * softplus
    beta = _sigmoid(ab)
    g1 = g.astype(BF16)
    r1 = g - g1.astype(F32)
    g2 = r1.astype(BF16)
    g3 = (r1 - g2.astype(F32)).astype(BF16)
    gs = jnp.concatenate([g1, g2, g3], axis=1)
    cf = _dot(lower_ref[...], gs)
    cb = _dot(upper_ref[...], gs)
    cum_f = cf[:, 0:hd] + cf[:, hd:2 * hd] + cf[:, 2 * hd:3 * hd]
    cum_b = cb[:, 0:hd] + cb[:, hd:2 * hd] + cb[:, 2 * hd:3 * hd]
    gates_ref[...] = jnp.where(lane < nh, cum_f, jnp.where((lane >= 2 * nh) & (lane < 3 * nh), cum_b, beta))


def _gdn_inproj(h, gain, w_in, conv_w, conv_b, alog_row, dtb_row, seq):
    t, d = h.shape
    tm = TM_GDN_IN
    hd = HEAD_DIM
    qk_w, v_w = GDN_QK_HEADS * hd, GDN_V_HEADS * hd
    n_tiles = t // tm
    row = lambda i: (i, 0)
    r = lax.broadcasted_iota(jnp.int32, (tm, tm), 0)
    c = lax.broadcasted_iota(jnp.int32, (tm, tm), 1)
    same = (r // GDN_CHUNK) == (c // GDN_CHUNK)
    lower = (same & (c <= r)).astype(BF16)
    upper = (same & (c >= r)).astype(BF16)
    return pl.pallas_call(
        functools.partial(_gdn_inproj_kernel, seq_tiles=seq // tm),
        grid=(n_tiles,),
        in_specs=_halo_specs(tm, d, n_tiles) + [
            _const_spec((1, d)),
            _const_spec(w_in.shape),
            _const_spec(conv_w.shape),
            _const_spec(conv_b.shape),
            _const_spec((1, hd)),
            _const_spec((1, hd)),
            _const_spec((tm, tm)),
            _const_spec((tm, tm)),
        ],
        out_specs=[
            pl.BlockSpec((tm, qk_w), row),
            pl.BlockSpec((tm, qk_w), row),
            pl.BlockSpec((tm, v_w), row),
            pl.BlockSpec((tm, v_w), row),
            pl.BlockSpec((tm, hd), row),
        ],
        out_shape=[
            jax.ShapeDtypeStruct((t, qk_w), BF16),
            jax.ShapeDtypeStruct((t, qk_w), BF16),
            jax.ShapeDtypeStruct((t, v_w), BF16),
            jax.ShapeDtypeStruct((t, v_w), BF16),
            jax.ShapeDtypeStruct((t, hd), F32),
        ],
        compiler_params=_params("parallel"),
        name="gdn_inproj",
    )(h, h, h, gain, w_in, conv_w, conv_b, alog_row, dtb_row, lower, upper)


def _unit_triangular_inverses(l_mats, row, col):
    n = l_mats[0].shape[0]
    size = GDN_DIAG
    same = (row // size) == (col // size)
    eye = jnp.where(row == col, 1.0, 0.0)
    lds = [jnp.where(same, l, 0.0) for l in l_mats]
    ldbs = [ld.astype(BF16) for ld in lds]
    qs = [_dot(ldb, ldb) for ldb in ldbs]
    ps = [eye - ld for ld in lds]
    power = 2
    while 2 * power < size:
        qbs = [q.astype(BF16) for q in qs]
        xs = [_dot(qb, jnp.concatenate([qb, p.astype(BF16)], axis=1)) for qb, p in zip(qbs, ps)]
        qs = [x[:, 0:n] for x in xs]
        ps = [p + x[:, n:2 * n] for p, x in zip(ps, xs)]
        power *= 2
    ds = [p + _dot(q.astype(BF16), p.astype(BF16)) for q, p in zip(qs, ps)]
    while size < n:
        pair = (row // (2 * size)) == (col // (2 * size))
        join = pair & jnp.logical_not(same)
        dbs = [d.astype(BF16) for d in ds]
        es = [_dot(jnp.where(join, l, 0.0).astype(BF16), db).astype(BF16) for l, db in zip(l_mats, dbs)]
        ds = [d - _dot(db, e) for d, db, e in zip(ds, dbs, es)]
        same = pair
        size *= 2
    return ds


def _gdn_kernel(q_ref, k_ref, v_ref, gates_ref, gates_t_ref, o_ref, state_ref, *, reverse):
    hd = HEAD_DIM
    n = GDN_CHUNK
    nh = GDN_V_HEADS
    rep = GDN_V_HEADS // GDN_QK_HEADS

    @pl.when(pl.program_id(1) == 0)
    def _():
        state_ref[...] = jnp.zeros(state_ref.shape, F32)

    row = lax.broadcasted_iota(jnp.int32, (n, n), 0)
    col = lax.broadcasted_iota(jnp.int32, (n, n), 1)
    incl = (col >= row) if reverse else (col <= row)
    strict = (col > row) if reverse else (col < row)
    last = 0 if reverse else n - 1
    gc_lane = 2 * nh if reverse else 0
    beta_lane = 3 * nh if reverse else nh
    gates = gates_ref[...]

    for first in range(0, nh, GDN_HEAD_GROUP):
        heads = list(range(first, first + GDN_HEAD_GROUP))
        qk_heads = sorted({h // rep for h in heads})
        kq = {hq: k_ref[:, hq * hd:(hq + 1) * hd] for hq in qk_heads}
        qq = {hq: q_ref[:, hq * hd:(hq + 1) * hd] for hq in qk_heads}
        kk = {hq: _dot_nt(kq[hq], kq[hq]) for hq in qk_heads}
        qk = {hq: _dot_nt(qq[hq], kq[hq]) for hq in qk_heads}
        gcol, bcol, g_last, decay, l_mats = [], [], [], [], []
        for h in heads:
            gc = jnp.broadcast_to(gates[:, gc_lane + h:gc_lane + h + 1], (n, n))
            bc = jnp.broadcast_to(gates[:, beta_lane + h:beta_lane + h + 1], (n, n))
            grow = gates_t_ref[gc_lane + h:gc_lane + h + 1, :]
            dec = jnp.exp(jnp.where(incl, gc - grow, -1e30))
            gcol.append(gc)
            bcol.append(bc)
            g_last.append(grow[:, last:last + 1])
            decay.append(dec)
            l_mats.append(jnp.where(strict, kk[h // rep] * dec * bc, 0.0))
        t_inv = _unit_triangular_inverses(l_mats, row, col)
        uw, lhs_state, lhs_new = [], [], []
        for i, h in enumerate(heads):
            kf = kq[h // rep].astype(F32)
            qf = qq[h // rep].astype(F32)
            eg = jnp.exp(gcol[i])
            vf = v_ref[:, h * hd:(h + 1) * hd].astype(F32)
            rhs = jnp.concatenate([(vf * bcol[i]).astype(BF16), (kf * bcol[i] * eg).astype(BF16)], axis=1)
            uw.append(_dot(t_inv[i].astype(BF16), rhs))
            attn = (qk[h // rep] * decay[i]).astype(BF16)
            kd_t = (kf * jnp.exp(g_last[i] - gcol[i])).T.astype(BF16)
            lhs_state.append((qf * eg).astype(BF16))
            lhs_new.append(jnp.concatenate([attn, kd_t], axis=0))
        states = [state_ref[h] for h in heads]
        pq = [_dot(jnp.concatenate([x[:, hd:2 * hd].astype(BF16), qd], axis=0), s.astype(BF16))
              for x, qd, s in zip(uw, lhs_state, states)]
        v_new = [(x[:, 0:hd] - y[0:n]).astype(BF16) for x, y in zip(uw, pq)]
        r = [_dot(a, b) for a, b in zip(lhs_new, v_new)]
        for i, h in enumerate(heads):
            o_ref[:, h * hd:(h + 1) * hd] = (pq[i][n:2 * n] + r[i][0:n]).astype(BF16)
            state_ref[h] = states[i] * jnp.exp(g_last[i]) + r[i][n:2 * n]


def _gdn_scan(q, k, v, gates, gates_t, batch, seq, reverse):
    t = q.shape[0]
    hd = HEAD_DIM
    n = GDN_CHUNK
    nc = seq // n
    qk_w, v_w = GDN_QK_HEADS * hd, GDN_V_HEADS * hd
    blk = (lambda b, c: b * nc + nc - 1 - c) if reverse else (lambda b, c: b * nc + c)
    row = lambda b, c: (blk(b, c), 0)
    return pl.pallas_call(
        functools.partial(_gdn_kernel, reverse=reverse),
        grid=(batch, nc),
        in_specs=[
            pl.BlockSpec((n, qk_w), row),
            pl.BlockSpec((n, qk_w), row),
            pl.BlockSpec((n, v_w), row),
            pl.BlockSpec((n, hd), row),
            pl.BlockSpec((hd, n), lambda b, c: (0, blk(b, c))),
        ],
        out_specs=pl.BlockSpec((n, v_w), row),
        out_shape=jax.ShapeDtypeStruct((t, v_w), BF16),
        scratch_shapes=[pltpu.VMEM((GDN_V_HEADS, hd, hd), F32)],
        compiler_params=_params("parallel", "arbitrary"),
        name="gdn_scan_bwd" if reverse else "gdn_scan_fwd",
    )(q, k, v, gates, gates_t)


def _gdn_outproj_kernel(of_ref, ob_ref, z_ref, g_ref, w_ref, res_ref, o_ref, act_ref):
    hd = HEAD_DIM
    for h in range(GDN_V_HEADS):
        cols = slice(h * hd, (h + 1) * hd)
        o = _rms(of_ref[:, cols].astype(F32) + ob_ref[:, cols].astype(F32), g_ref[...])
        z = z_ref[:, cols].astype(F32)
        act_ref[:, cols] = (o * (z * _sigmoid(z))).astype(BF16)
    o_ref[...] = res_ref[...] + _dot(act_ref[...], w_ref[...])


def _gdn_outproj(o_f, o_b, z, gain, w, res):
    t, k = o_f.shape
    d = w.shape[1]
    tm = TM_PROJ
    row = lambda i: (i, 0)
    return pl.pallas_call(
        _gdn_outproj_kernel,
        grid=(t // tm,),
        in_specs=[
            pl.BlockSpec((tm, k), row),
            pl.BlockSpec((tm, k), row),
            pl.BlockSpec((tm, k), row),
            _const_spec((1, HEAD_DIM)),
            _const_spec((k, d)),
            pl.BlockSpec((tm, d), row),
        ],
        out_specs=pl.BlockSpec((tm, d), row),
        out_shape=jax.ShapeDtypeStruct((t, d), F32),
        scratch_shapes=[pltpu.VMEM((tm, k), BF16)],
        compiler_params=_params("parallel"),
        name="gdn_outproj",
    )(o_f, o_b, z, gain, w, res)


def _rope_tables(seq):
    grid_rows = seq // ROPE_GRID_W
    half = HEAD_DIM // 2
    n_freq = half // 2
    inv_freq = ROPE_THETA ** (-(jnp.arange(n_freq, dtype=F32) * 2.0 / half))
    ang_r = jnp.arange(grid_rows, dtype=F32)[:, None] * inv_freq[None, :]
    ang_c = jnp.arange(ROPE_GRID_W, dtype=F32)[:, None] * inv_freq[None, :]
    cos_r, sin_r, cos_c, sin_c = lax.optimization_barrier(
        (jnp.cos(ang_r), jnp.sin(ang_r), jnp.cos(ang_c), jnp.sin(ang_c)))
    by_row = lambda a: jnp.repeat(a, ROPE_GRID_W, axis=0)
    by_col = lambda a: jnp.tile(a, (grid_rows, 1))
    zero = jnp.zeros((seq, n_freq), F32)
    cos = jnp.concatenate([by_row(cos_r), by_row(cos_r), by_col(cos_c), by_col(cos_c)], axis=-1)
    sa = jnp.concatenate([-by_row(sin_r), zero, -by_col(sin_c), zero], axis=-1)
    sb = jnp.concatenate([zero, by_row(sin_r), zero, by_col(sin_c)], axis=-1)
    return cos, sa, sb


def kernel(x, norm_mix, norm_ffn, norm_final, attn_w_in, attn_q_norm, attn_k_norm, attn_w_out, gdn_w_in, gdn_conv_w, gdn_conv_b, gdn_a_log, gdn_dt_bias, gdn_o_norm, gdn_w_out, ffn_w_up, ffn_conv_w, ffn_conv_b, ffn_w_down):
    batch, seq, d = x.shape
    t = batch * seq
    hd = HEAD_DIM
    assert seq % TM_PROJ == 0 and seq % GDN_CHUNK == 0
    assert all(seq % (tk * unroll) == 0 for tk, unroll, _ in (ATT_BOUNDED_UNITS, ATT_ONLINE_UNITS))
    assert seq % ROPE_GRID_W == 0
    h = x.reshape(t, d)
    row = lambda a: a.reshape(1, -1)
    final_gain = row(norm_final)

    rope = _rope_tables(seq)
    q_gain = row(attn_q_norm[0]) * (hd ** -0.5 * LOG2E)
    q, k, v, gate = _attn_inproj(h, row(norm_mix[0]), attn_w_in[0].astype(BF16), q_gain,
                                 row(attn_k_norm[0]), rope, seq)
    vt = v.reshape(batch, seq, ATT_KV_HEADS, hd).transpose(0, 2, 3, 1).reshape(batch * ATT_KV_HEADS, hd, seq)
    bound = hd * jnp.max(jnp.abs(q_gain)) * jnp.max(jnp.abs(attn_k_norm[0])) * ATT_BOUND_SLACK
    shift = jnp.stack([bound, (bound <= ATT_MAX_SHIFT).astype(F32)])
    attn = _flash_attention(shift, q, k, vt, gate, batch, seq)
    h = _outproj(attn, attn_w_out[0].astype(BF16), h)
    ffn = (ffn_w_up.astype(BF16), ffn_conv_w, ffn_conv_b[:, None, :], ffn_w_down.astype(BF16), final_gain, seq)
    h = _ffn(h, row(norm_ffn[0]), *ffn, 0, False)

    pad_gate = lambda p: jnp.pad(jnp.stack([p[0], jnp.zeros_like(p[0]), p[1], jnp.zeros_like(p[1])]).reshape(1, -1),
                                 ((0, 0), (0, hd - 4 * GDN_V_HEADS)))
    gq, gk, gv, gz, gates = _gdn_inproj(h, row(norm_mix[1]), gdn_w_in[0].astype(BF16), gdn_conv_w[0],
                                        row(gdn_conv_b[0]), pad_gate(gdn_a_log[0]), pad_gate(gdn_dt_bias[0]), seq)
    gates_t = gates.T
    o_f = _gdn_scan(gq, gk, gv, gates, gates_t, batch, seq, False)
    o_b = _gdn_scan(gq, gk, gv, gates, gates_t, batch, seq, True)
    h = _gdn_outproj(o_f, o_b, gz, row(gdn_o_norm[0]), gdn_w_out[0].astype(BF16), h)
    h = _ffn(h, row(norm_ffn[1]), *ffn, 1, True)
    return h.reshape(batch, seq, d)
```
